```python
import math, functools
import jax, jax.numpy as jnp
from jax import lax
import numpy as np

D_MODEL = 1024
BATCH = 2
SEQ = 8192
DEPTH = 1
DEC_BATCH = 32
DEC_SEQ = 8
PAST_LEN = 8192
PAGE_SIZE = 128

N_HEADS = 8
HEAD_DIM = 64
V_DIM = 2 * HEAD_DIM
QK_WIDTH = N_HEADS * 2 * HEAD_DIM
ATTN_WIDTH = N_HEADS * V_DIM
POOL_WINDOWS = (2, 4, 8, 16)
N_POOL_GROUPS = len(POOL_WINDOWS)
POOL_WIDTH = 1024
POOL_GROUP = POOL_WIDTH // N_POOL_GROUPS
POOL_BUF = max(POOL_WINDOWS) - 1
D_FF = 2816
CONV_W = 3
FFN_BUF = CONV_W - 1
Q_BLOCK = 128
EPS = 1e-6
NEG = -1e30
IN_WIDTH = 2 * QK_WIDTH + ATTN_WIDTH + POOL_WIDTH + 2 * D_MODEL
SPLITS = (QK_WIDTH, 2 * QK_WIDTH, 2 * QK_WIDTH + ATTN_WIDTH,
          2 * QK_WIDTH + ATTN_WIDTH + POOL_WIDTH,
          2 * QK_WIDTH + ATTN_WIDTH + POOL_WIDTH + D_MODEL)

kernel_name = "diffattn_pool_gated_hybrid_step"


def lambda_init_for(layer):
    return 0.8 - 0.6 * math.exp(-0.3 * layer)


def rms_norm(x, g):
    xf = x.astype(jnp.float32)
    y = xf * lax.rsqrt(jnp.mean(xf * xf, axis=-1, keepdims=True) + EPS)
    return (y * g.astype(jnp.float32)).astype(x.dtype)


def diff_probs(scores, lam):
    p = jax.nn.softmax(scores, axis=-1)
    return p[:, :, 0] - lam * p[:, :, 1]


def prompt_attention(q, k, v, lam):
    b, t = q.shape[:2]
    nb = t // Q_BLOCK
    qb = q.reshape(b, nb, Q_BLOCK, N_HEADS, 2, HEAD_DIM).transpose(1, 0, 2, 3, 4, 5)
    kpos = jnp.arange(t)
    scale = HEAD_DIM ** -0.5

    def one_block(args):
        i, qi = args
        qpos = i * Q_BLOCK + jnp.arange(Q_BLOCK)
        s = jnp.einsum('bqhcd,bkhcd->bhcqk', qi, k, preferred_element_type=jnp.float32) * scale
        s = jnp.where(kpos[None, :] <= qpos[:, None], s, NEG)
        p = diff_probs(s, lam).astype(v.dtype)
        return jnp.einsum('bhqk,bkhe->bqhe', p, v)

    o = lax.map(one_block, (jnp.arange(nb), qb))
    return o.transpose(1, 0, 2, 3, 4).reshape(b, t, N_HEADS, V_DIM)


def sample_attention(q, k_new, v_new, lam, k_cache, v_cache, page_table, layer):
    b, t = q.shape[:2]
    k_past = k_cache[layer, page_table].reshape(b, -1, N_HEADS, 2, HEAD_DIM)
    v_past = v_cache[layer, page_table].reshape(b, -1, N_HEADS, V_DIM)
    n_past = k_past.shape[1]
    scale = HEAD_DIM ** -0.5
    s_past = jnp.einsum('bqhcd,bkhcd->bhcqk', q, k_past, preferred_element_type=jnp.float32)
    s_new = jnp.einsum('bqhcd,bkhcd->bhcqk', q, k_new, preferred_element_type=jnp.float32)
    s_new = jnp.where(jnp.tril(jnp.ones((t, t), dtype=bool)), s_new, NEG)
    s = jnp.concatenate([s_past, s_new], axis=-1) * scale
    p = diff_probs(s, lam).astype(v_new.dtype)
    return (jnp.einsum('bhqk,bkhe->bqhe', p[..., :n_past], v_past)
            + jnp.einsum('bhqk,bkhe->bqhe', p[..., n_past:], v_new))


def pool_mixer(u, prefix, p0, w_pool, pool_scale):
    b, t, _ = u.shape
    ext_raw = jnp.concatenate([prefix, u], axis=1)
    ext = ext_raw.astype(jnp.float32)
    cs = jnp.concatenate([jnp.zeros((b, 1, POOL_WIDTH), jnp.float32), jnp.cumsum(ext, axis=1)], axis=1)
    pos = p0 + jnp.arange(t)
    outs = []
    for g, w in enumerate(POOL_WINDOWS):
        c0, c1 = g * POOL_GROUP, (g + 1) * POOL_GROUP
        hi = cs[:, POOL_BUF + 1:POOL_BUF + 1 + t, c0:c1]
        lo = cs[:, POOL_BUF + 1 - w:POOL_BUF + 1 - w + t, c0:c1]
        cnt = jnp.minimum(w, pos + 1).astype(jnp.float32)[None, :, None]
        d = ((hi - lo) / cnt - ext[:, POOL_BUF:, c0:c1]).astype(u.dtype)
        outs.append(jnp.einsum('btc,cd->btd', d, w_pool[g]) * pool_scale[g])
    return jnp.concatenate(outs, axis=-1), ext_raw[:, -POOL_BUF:]


def conv_ffn(h, prefix, w_up, conv_w, conv_b, w_down):
    t = h.shape[1]
    up = h @ w_up
    ext = jnp.concatenate([prefix, up], axis=1)
    c = conv_b + sum(ext[:, j:j + t] * conv_w[j] for j in range(CONV_W))
    gate, val = jnp.split(c, 2, axis=-1)
    return (jax.nn.silu(gate) * val) @ w_down, ext[:, -FFN_BUF:]


def decoder_layer(x, attention_fn, pool_prefix, ffn_prefix, p0, lam_init, lp):
    (norm_mix, w_in, q_norm, k_norm, lq1, lk1, lq2, lk2, head_norm, w_pool, pool_scale,
     w_attn_branch, w_pool_branch, w_out, norm_ffn, w_up, conv_w, conv_b, w_down) = lp
    b, t, _ = x.shape
    h = rms_norm(x, norm_mix)
    q, k, v, u, ga, gp = jnp.split(h @ w_in, SPLITS, axis=-1)
    q = rms_norm(q.reshape(b, t, N_HEADS, 2, HEAD_DIM), q_norm)
    k = rms_norm(k.reshape(b, t, N_HEADS, 2, HEAD_DIM), k_norm)
    v = v.reshape(b, t, N_HEADS, V_DIM)
    f32 = jnp.float32
    lam = (jnp.exp(jnp.sum(lq1.astype(f32) * lk1.astype(f32)))
           - jnp.exp(jnp.sum(lq2.astype(f32) * lk2.astype(f32))) + lam_init)
    o = attention_fn(q, k, v, lam)
    o = (rms_norm(o, head_norm) * (1.0 - lam_init)).reshape(b, t, ATTN_WIDTH)
    pool_o, pool_state = pool_mixer(u, pool_prefix, p0, w_pool, pool_scale)
    merged = jax.nn.sigmoid(ga) * (o @ w_attn_branch) + jax.nn.sigmoid(gp) * (pool_o @ w_pool_branch)
    x1 = x + merged @ w_out
    f, ffn_state = conv_ffn(rms_norm(x1, norm_ffn), ffn_prefix, w_up, conv_w, conv_b, w_down)
    return x1 + f, k.reshape(b, t, N_HEADS, 2 * HEAD_DIM), v, pool_state, ffn_state


def setup_inputs(seed: int = 0) -> dict:
    key = jax.random.key(seed)
    ks = jax.random.split(key, 32)
    n_pages = PAST_LEN // PAGE_SIZE
    n_used = DEC_BATCH * n_pages
    n_phys = n_used + max(1, n_used // 4)
    nrm = jax.random.normal
    f = jnp.float32
    page_table = jax.random.permutation(ks[0], n_phys)[:n_used].reshape(DEC_BATCH, n_pages).astype(jnp.int32)
    return {
        'x_prompt': nrm(ks[1], (BATCH, SEQ, D_MODEL), f),
        'x_sample': nrm(ks[2], (DEC_BATCH, DEC_SEQ, D_MODEL), f),
        'cache_k': nrm(ks[3], (DEPTH, n_phys, PAGE_SIZE, N_HEADS, 2 * HEAD_DIM), f),
        'cache_v': nrm(ks[4], (DEPTH, n_phys, PAGE_SIZE, N_HEADS, V_DIM), f),
        'state_pool': nrm(ks[5], (DEPTH, DEC_BATCH, POOL_BUF, POOL_WIDTH), f),
        'state_ffn_conv': nrm(ks[6], (DEPTH, DEC_BATCH, FFN_BUF, 2 * D_FF), f),
        'page_table': page_table,
        'norm_mix': 1.0 + 0.02 * nrm(ks[7], (DEPTH, D_MODEL), f),
        'w_in': nrm(ks[8], (DEPTH, D_MODEL, IN_WIDTH), f) * D_MODEL ** -0.5,
        'q_norm': 1.0 + 0.02 * nrm(ks[9], (DEPTH, 2, HEAD_DIM), f),
        'k_norm': 1.0 + 0.02 * nrm(ks[10], (DEPTH, 2, HEAD_DIM), f),
        'lambda_q1': 0.1 * nrm(ks[11], (DEPTH, HEAD_DIM), f),
        'lambda_k1': 0.1 * nrm(ks[12], (DEPTH, HEAD_DIM), f),
        'lambda_q2': 0.1 * nrm(ks[13], (DEPTH, HEAD_DIM), f),
        'lambda_k2': 0.1 * nrm(ks[14], (DEPTH, HEAD_DIM), f),
        'head_norm': 1.0 + 0.02 * nrm(ks[15], (DEPTH, N_HEADS, V_DIM), f),
        'w_pool': nrm(ks[16], (DEPTH, N_POOL_GROUPS, POOL_GROUP, POOL_GROUP), f) * POOL_GROUP ** -0.5,
        'pool_scale': 1.0 + 0.1 * nrm(ks[17], (DEPTH, N_POOL_GROUPS, POOL_GROUP), f),
        'w_attn_branch': nrm(ks[18], (DEPTH, ATTN_WIDTH, D_MODEL), f) * ATTN_WIDTH ** -0.5,
        'w_pool_branch': nrm(ks[19], (DEPTH, POOL_WIDTH, D_MODEL), f) * POOL_WIDTH ** -0.5,
        'w_out': nrm(ks[20], (DEPTH, D_MODEL, D_MODEL), f) * D_MODEL ** -0.5,
        'norm_ffn': 1.0 + 0.02 * nrm(ks[21], (DEPTH, D_MODEL), f),
        'w_up': nrm(ks[22], (DEPTH, D_MODEL, 2 * D_FF), f) * D_MODEL ** -0.5,
        'conv_w': nrm(ks[23], (DEPTH, CONV_W, 2 * D_FF), f) * CONV_W ** -0.5,
        'conv_b': 0.02 * nrm(ks[24], (DEPTH, 2 * D_FF), f),
        'w_down': nrm(ks[25], (DEPTH, D_FF, D_MODEL), f) * D_FF ** -0.5,
    }


def reference(x_prompt, x_sample, cache_k, cache_v, state_pool, state_ffn_conv, page_table,
              norm_mix, w_in, q_norm, k_norm, lambda_q1, lambda_k1, lambda_q2, lambda_k2,
              head_norm, w_pool, pool_scale, w_attn_branch, w_pool_branch, w_out,
              norm_ffn, w_up, conv_w, conv_b, w_down):
    yp, ys = x_prompt, x_sample
    kp_l, vp_l, pp_l, fp_l, ks_l, vs_l, ps_l, fs_l = [], [], [], [], [], [], [], []
    for l in range(DEPTH):
        lam_init = lambda_init_for(l)
        lp = (norm_mix[l], w_in[l], q_norm[l], k_norm[l], lambda_q1[l], lambda_k1[l],
              lambda_q2[l], lambda_k2[l], head_norm[l], w_pool[l], pool_scale[l],
              w_attn_branch[l], w_pool_branch[l], w_out[l], norm_ffn[l], w_up[l],
              conv_w[l], conv_b[l], w_down[l])
        pool_prefix = jnp.zeros((yp.shape[0], POOL_BUF, POOL_WIDTH), yp.dtype)
        ffn_prefix = jnp.zeros((yp.shape[0], FFN_BUF, 2 * D_FF), yp.dtype)
        yp, kp, vp, pp, fp = decoder_layer(yp, prompt_attention, pool_prefix, ffn_prefix, 0, lam_init, lp)
        attn_s = functools.partial(sample_attention, k_cache=cache_k, v_cache=cache_v,
                                   page_table=page_table, layer=l)
        ys, k_s, v_s, p_s, f_s = decoder_layer(ys, attn_s, state_pool[l], state_ffn_conv[l],
                                              PAST_LEN, lam_init, lp)
        kp_l.append(kp); vp_l.append(vp); pp_l.append(pp); fp_l.append(fp)
        ks_l.append(k_s); vs_l.append(v_s); ps_l.append(p_s); fs_l.append(f_s)
    return (yp, ys,
            jnp.stack(kp_l), jnp.stack(vp_l), jnp.stack(pp_l), jnp.stack(fp_l),
            jnp.stack(ks_l), jnp.stack(vs_l), jnp.stack(ps_l), jnp.stack(fs_l))
```

```python
import functools
import math

import jax
import jax.numpy as jnp
from jax import lax
from jax.experimental import pallas as pl
from jax.experimental.pallas import tpu as pltpu

F32 = jnp.float32
BF16 = jnp.bfloat16

N_HEADS = 8
HEAD_DIM = 64
V_DIM = 2 * HEAD_DIM
POOL_WINDOWS = (2, 4, 8, 16)
POOL_GROUP = 256
POOL_BUF = max(POOL_WINDOWS) - 1
POOL_HALO = 16
CONV_W = 3
CONV_HALO = 8
PAGE_SIZE = 128
EPS = 1e-6
NEG = -1e30
FF_CHUNK = 256
IN_SECTION = 1024
VMEM_LIMIT = 56 * 1024 * 1024


def lambda_init_for(layer):
    return 0.8 - 0.6 * math.exp(-0.3 * layer)


def _params(*sem):
    return pltpu.CompilerParams(dimension_semantics=sem, vmem_limit_bytes=VMEM_LIMIT)


def _sigmoid(x):
    return 1.0 / (1.0 + jnp.exp(-x))


def _rms_rows(x, g):
    return x * lax.rsqrt(jnp.mean(x * x, axis=-1, keepdims=True) + EPS) * g


def _lambda_value(lam_ref, lam_init):
    lv = lam_ref[...]
    s1 = jnp.sum(lv[0:1] * lv[1:2], axis=-1, keepdims=True)
    s2 = jnp.sum(lv[2:3] * lv[3:4], axis=-1, keepdims=True)
    return jnp.exp(s1) - jnp.exp(s2) + lam_init


def _inproj_kernel(x_ref, nm_ref, w_ref, qg_ref, kg_ref,
                   q_ref, kf_ref, kb_ref, vf_ref, vb_ref, u_ref, sa_ref, sp_ref, h_ref):
    j = pl.program_id(1)

    @pl.when(j == 0)
    def _():
        h_ref[...] = _rms_rows(x_ref[...], nm_ref[...]).astype(BF16)

    y = jnp.dot(h_ref[...], w_ref[...], preferred_element_type=F32)

    def group_norm(g_ref):
        r = lax.broadcasted_iota(jnp.int32, (V_DIM, V_DIM), 0) // HEAD_DIM
        c = lax.broadcasted_iota(jnp.int32, (V_DIM, V_DIM), 1) // HEAD_DIM
        ones_bd = (r == c).astype(BF16)
        parts = []
        for h in range(N_HEADS):
            yh = y[:, h * V_DIM:(h + 1) * V_DIM]
            ss = jnp.dot((yh * yh).astype(BF16), ones_bd, preferred_element_type=F32)
            parts.append(yh * lax.rsqrt(ss * (1.0 / HEAD_DIM) + EPS) * g_ref[...])
        return jnp.concatenate(parts, axis=-1)

    @pl.when(j == 0)
    def _():
        q_ref[...] = (group_norm(qg_ref) * (HEAD_DIM ** -0.5)).astype(BF16)

    def store_heads(dst_ref, val):
        for h in range(N_HEADS):
            dst_ref[:, h, :] = val[:, h * V_DIM:(h + 1) * V_DIM]

    @pl.when(j == 1)
    def _():
        kn = group_norm(kg_ref)
        store_heads(kf_ref, kn)
        kb_ref[...] = kn.astype(BF16)

    @pl.when(j == 2)
    def _():
        store_heads(vf_ref, y)
        vb_ref[...] = y.astype(BF16)

    @pl.when(j == 3)
    def _():
        u_ref[...] = y

    @pl.when(j == 4)
    def _():
        sa_ref[...] = _sigmoid(y)

    @pl.when(j == 5)
    def _():
        sp_ref[...] = _sigmoid(y)


def _in_projection(x2d, norm_mix, w_in_bf, q_gain, k_gain, tm):
    n, d = x2d.shape
    n_sections = w_in_bf.shape[1] // IN_SECTION
    row = lambda i, j: (i, 0)
    fixed = lambda i, j: (0, 0)
    tile = pl.BlockSpec((tm, IN_SECTION), row)
    heads_tile = pl.BlockSpec((tm, N_HEADS, V_DIM), lambda i, j: (i, 0, 0))
    flat = lambda dt: jax.ShapeDtypeStruct((n, IN_SECTION), dt)
    heads = jax.ShapeDtypeStruct((n, N_HEADS, V_DIM), F32)
    out_specs = [tile, heads_tile, tile, heads_tile, tile, tile, tile, tile]
    out_shape = [flat(BF16), heads, flat(BF16), heads, flat(BF16), flat(F32), flat(F32), flat(F32)]
    return pl.pallas_call(
        _inproj_kernel,
        grid=(n // tm, n_sections),
        in_specs=[pl.BlockSpec((tm, d), row),
                  pl.BlockSpec((1, d), fixed),
                  pl.BlockSpec((d, IN_SECTION), lambda i, j: (0, j)),
                  pl.BlockSpec((1, V_DIM), fixed),
                  pl.BlockSpec((1, V_DIM), fixed)],
        out_specs=out_specs,
        out_shape=out_shape,
        scratch_shapes=[pltpu.VMEM((tm, d), BF16)],
        compiler_params=_params("arbitrary", "arbitrary"),
        name="in_projection",
    )(x2d, norm_mix, w_in_bf, q_gain, k_gain)


def _prompt_attn_kernel(lam_ref, q_ref, k_ref, v_ref, hn_ref, o_ref, m_ref, l_ref, acc_ref, *, blk, lam_init):
    i = pl.program_id(2)
    q = q_ref[0]
    lane = lax.broadcasted_iota(jnp.int32, q.shape, 1)
    zero = jnp.zeros_like(q)
    q_comp = (jnp.where(lane < HEAD_DIM, q, zero), jnp.where(lane >= HEAD_DIM, q, zero))

    m_ref[...] = jnp.full(m_ref.shape, NEG, F32)
    l_ref[...] = jnp.zeros(l_ref.shape, F32)
    acc_ref[...] = jnp.zeros(acc_ref.shape, F32)

    def update(j, masked):
        start = pl.multiple_of(j * blk, blk)
        k = k_ref[0, pl.ds(start, blk), :]
        v = v_ref[0, pl.ds(start, blk), :]
        for c in range(2):
            s = lax.dot_general(q_comp[c], k, (((1,), (1,)), ((), ())), preferred_element_type=F32)
            if masked:
                row = lax.broadcasted_iota(jnp.int32, s.shape, 0)
                col = lax.broadcasted_iota(jnp.int32, s.shape, 1)
                s = jnp.where(col <= row, s, NEG)
            m_old = m_ref[c]
            m_new = jnp.maximum(m_old, jnp.max(s, axis=-1, keepdims=True))
            p = jnp.exp(s - m_new)
            alpha = jnp.exp(m_old - m_new)
            l_ref[c] = alpha * l_ref[c] + jnp.sum(p, axis=-1, keepdims=True)
            acc_ref[c] = alpha * acc_ref[c] + jnp.dot(p.astype(BF16), v, preferred_element_type=F32)
            m_ref[c] = m_new

    def body(j, carry):
        update(j, False)
        return carry

    lax.fori_loop(0, i, body, 0)
    update(i, True)

    lam = _lambda_value(lam_ref, lam_init)
    o = acc_ref[0] / l_ref[0] - lam * (acc_ref[1] / l_ref[1])
    o_ref[0] = (_rms_rows(o, hn_ref[0]) * (1.0 - lam_init)).astype(BF16)


def _prompt_attention(lam_vecs, q, k, v, head_norm, lam_init, blk):
    b, t, _ = q.shape
    kv_spec = pl.BlockSpec((1, t, V_DIM), lambda bi, h, i: (bi, 0, h))
    q_spec = pl.BlockSpec((1, blk, V_DIM), lambda bi, h, i: (bi, i, h))
    return pl.pallas_call(
        functools.partial(_prompt_attn_kernel, blk=blk, lam_init=lam_init),
        grid=(b, N_HEADS, t // blk),
        in_specs=[pl.BlockSpec(lam_vecs.shape, lambda bi, h, i: (0, 0)),
                  q_spec, kv_spec, kv_spec,
                  pl.BlockSpec((1, 1, V_DIM), lambda bi, h, i: (h, 0, 0))],
        out_specs=q_spec,
        out_shape=jax.ShapeDtypeStruct(q.shape, BF16),
        scratch_shapes=[pltpu.VMEM((2, blk, 1), F32), pltpu.VMEM((2, blk, 1), F32),
                        pltpu.VMEM((2, blk, V_DIM), F32)],
        compiler_params=_params("arbitrary", "arbitrary", "arbitrary"),
        name="prompt_attention",
    )(lam_vecs, q, k, v, head_norm.reshape(N_HEADS, 1, V_DIM))


def _decode_attn_kernel(pt_ref, lam_ref, q_ref, kn_ref, vn_ref, ck_ref, cv_ref, hn_ref, o_ref,
                        a_ref, m_ref, l_ref, acc_ref, *, n_new, lam_init):
    del pt_ref
    p_idx = pl.program_id(1)
    n_cols = 2 * N_HEADS * n_new
    cols_per_head = 2 * n_new

    def head_of(shape, row_axis, col_axis):
        r = lax.broadcasted_iota(jnp.int32, shape, row_axis)
        c = lax.broadcasted_iota(jnp.int32, shape, col_axis)
        return r, c, c // cols_per_head

    def update(k2d, v2d, causal):
        n_tok = k2d.shape[0] // N_HEADS
        s = lax.dot_general(k2d, a_ref[...], (((1,), (1,)), ((), ())), preferred_element_type=F32)
        s3 = s.reshape(n_tok, N_HEADS, n_cols)
        h_idx, c_idx, c_head = head_of((N_HEADS, n_cols), 0, 1)
        valid = (h_idx == c_head)[None]
        if causal:
            t_idx = lax.broadcasted_iota(jnp.int32, s3.shape, 0)
            q_idx = lax.broadcasted_iota(jnp.int32, s3.shape, 2) % n_new
            valid = valid & (t_idx <= q_idx)
        m_old = m_ref[...]
        m_new = jnp.maximum(m_old, jnp.max(jnp.where(valid, s3, NEG), axis=0))
        p3 = jnp.where(valid, jnp.exp(s3 - m_new[None]), 0.0)
        alpha = jnp.exp(m_old - m_new)
        l_ref[...] = alpha * l_ref[...] + jnp.sum(p3, axis=0)
        m_ref[...] = m_new
        alpha_row = jnp.sum(jnp.where(h_idx == c_head, alpha, 0.0), axis=0, keepdims=True)
        pv = lax.dot_general(v2d, p3.reshape(n_tok * N_HEADS, n_cols), (((0,), (0,)), ((), ())),
                             preferred_element_type=F32)
        acc_ref[...] = acc_ref[...] * alpha_row + pv

    @pl.when(p_idx == 0)
    def _():
        q = q_ref[0].astype(F32)
        lane = lax.broadcasted_iota(jnp.int32, (n_new, V_DIM), 1)
        rows = []
        for h in range(N_HEADS):
            qh = q[:, h * V_DIM:(h + 1) * V_DIM]
            rows += [jnp.where(lane < HEAD_DIM, qh, 0.0), jnp.where(lane >= HEAD_DIM, qh, 0.0)]
        a_ref[...] = jnp.concatenate(rows, axis=0)
        m_ref[...] = jnp.full(m_ref.shape, NEG, F32)
        l_ref[...] = jnp.zeros(l_ref.shape, F32)
        acc_ref[...] = jnp.zeros(acc_ref.shape, F32)
        update(kn_ref[0], vn_ref[0], True)

    update(ck_ref[0].reshape(PAGE_SIZE * N_HEADS, V_DIM), cv_ref[0].reshape(PAGE_SIZE * N_HEADS, V_DIM), False)

    @pl.when(p_idx == pl.num_programs(1) - 1)
    def _():
        lam = _lambda_value(lam_ref, lam_init)
        h_idx, c_idx, c_head = head_of((N_HEADS, n_cols), 0, 1)
        l_row = jnp.sum(jnp.where(h_idx == c_head, l_ref[...], 0.0), axis=0, keepdims=True)
        on = (acc_ref[...] / l_row).T
        parts = []
        for h in range(N_HEADS):
            blk = on[h * cols_per_head:(h + 1) * cols_per_head]
            o = blk[:n_new] - lam * blk[n_new:]
            parts.append(_rms_rows(o, hn_ref[:, h * V_DIM:(h + 1) * V_DIM]) * (1.0 - lam_init))
        o_ref[0] = jnp.concatenate(parts, axis=-1).astype(BF16)


def _decode_attention(page_table, lam_vecs, q, k_new, v_new, cache_k, cache_v, head_norm, lam_init):
    b, n_new, width = q.shape
    n_pages = page_table.shape[1]
    n_cols = 2 * N_HEADS * n_new
    seq_spec = pl.BlockSpec((1, n_new, width), lambda bi, p, pt: (bi, 0, 0))
    new_spec = pl.BlockSpec((1, n_new * N_HEADS, V_DIM), lambda bi, p, pt: (bi, 0, 0))
    page_spec = pl.BlockSpec((1, PAGE_SIZE, N_HEADS, V_DIM), lambda bi, p, pt: (pt[bi, p], 0, 0, 0))
    grid_spec = pltpu.PrefetchScalarGridSpec(
        num_scalar_prefetch=1,
        grid=(b, n_pages),
        in_specs=[pl.BlockSpec(lam_vecs.shape, lambda bi, p, pt: (0, 0)),
                  seq_spec, new_spec, new_spec, page_spec, page_spec,
                  pl.BlockSpec((1, width), lambda bi, p, pt: (0, 0))],
        out_specs=seq_spec,
        scratch_shapes=[pltpu.VMEM((n_cols, V_DIM), F32), pltpu.VMEM((N_HEADS, n_cols), F32),
                        pltpu.VMEM((N_HEADS, n_cols), F32), pltpu.VMEM((V_DIM, n_cols), F32)],
    )
    return pl.pallas_call(
        functools.partial(_decode_attn_kernel, n_new=n_new, lam_init=lam_init),
        grid_spec=grid_spec,
        out_shape=jax.ShapeDtypeStruct(q.shape, BF16),
        compiler_params=_params("arbitrary", "arbitrary"),
        name="decode_attention",
    )(page_table, lam_vecs, q, k_new, v_new, cache_k, cache_v, head_norm.reshape(1, width))


def _pool_branch(ext, n_seq, seq_len, pos, wp_ref, ps_ref):
    rows = POOL_HALO + seq_len

    def tokens(a):
        if n_seq == 1:
            return a[POOL_HALO:]
        return a.reshape(n_seq, rows, a.shape[-1])[:, POOL_HALO:, :].reshape(n_seq * seq_len, a.shape[-1])

    cur = ext
    outs = []
    for g, w in enumerate(POOL_WINDOWS):
        cur = cur + pltpu.roll(cur, w // 2, axis=0)
        c0 = g * POOL_GROUP
        win = tokens(cur[:, :POOL_GROUP])
        tok = tokens(ext[:, c0:c0 + POOL_GROUP])
        cnt = jnp.minimum(w, pos + 1).astype(F32)
        d = (win / cnt - tok).astype(BF16)
        outs.append(jnp.dot(d, wp_ref[g], preferred_element_type=F32) * ps_ref[:, c0:c0 + POOL_GROUP])
        cur = cur[:, POOL_GROUP:]
    return jnp.concatenate(outs, axis=-1)


def _merge_project(x, o_n, pool_o, sig_a, sig_p, wa_ref, wpb_ref, wo_ref):
    attn = jnp.dot(o_n, wa_ref[...], preferred_element_type=F32)
    pool = jnp.dot(pool_o.astype(BF16), wpb_ref[...], preferred_element_type=F32)
    merged = sig_a * attn + sig_p * pool
    return x + jnp.dot(merged.astype(BF16), wo_ref[...], preferred_element_type=F32)


def _prompt_mix_kernel(u_ref, halo_ref, on_ref, sa_ref, sp_ref, x_ref, wp_ref, ps_ref, wa_ref, wpb_ref, wo_ref,
                       x1_ref, *, tm):
    i = pl.program_id(1)
    halo = jnp.where(i > 0, halo_ref[0], 0.0)
    ext = jnp.concatenate([halo, u_ref[0]], axis=0)
    pos = i * tm + lax.broadcasted_iota(jnp.int32, (tm, 1), 0)
    pool_o = _pool_branch(ext, 1, tm, pos, wp_ref, ps_ref)
    x1_ref[0] = _merge_project(x_ref[0], on_ref[0], pool_o, sa_ref[0], sp_ref[0], wa_ref, wpb_ref, wo_ref)


def _prompt_mix(u, o_n, sig_a, sig_p, x, w_pool, pool_scale, w_attn, w_poolb, w_out, tm):
    b, t, d = x.shape
    tile = lambda bi, i: (bi, i, 0)
    fixed2 = lambda bi, i: (0, 0)
    halo_blocks = tm // POOL_HALO
    tile_spec = pl.BlockSpec((1, tm, d), tile)
    return pl.pallas_call(
        functools.partial(_prompt_mix_kernel, tm=tm),
        grid=(b, t // tm),
        in_specs=[tile_spec,
                  pl.BlockSpec((1, POOL_HALO, d), lambda bi, i: (bi, jnp.maximum(i * halo_blocks - 1, 0), 0)),
                  tile_spec, tile_spec, tile_spec, tile_spec,
                  pl.BlockSpec(w_pool.shape, lambda bi, i: (0, 0, 0)),
                  pl.BlockSpec((1, d), fixed2),
                  pl.BlockSpec(w_attn.shape, fixed2),
                  pl.BlockSpec(w_poolb.shape, fixed2),
                  pl.BlockSpec(w_out.shape, fixed2)],
        out_specs=tile_spec,
        out_shape=jax.ShapeDtypeStruct(x.shape, F32),
        compiler_params=_params("arbitrary", "arbitrary"),
        name="prompt_mix",
    )(u, u, o_n, sig_a, sig_p, x, w_pool, pool_scale, w_attn, w_poolb, w_out)


def _sample_mix_kernel(u_ref, hist_ref, on_ref, sa_ref, sp_ref, x_ref, wp_ref, ps_ref, wa_ref, wpb_ref, wo_ref,
                       x1_ref, state_ref, *, n_seq, seq_len, pos0):
    d = u_ref.shape[-1]
    rows = POOL_HALO + seq_len
    ext3 = jnp.concatenate([hist_ref[...], u_ref[...].reshape(n_seq, seq_len, d)], axis=1)
    state_ref[...] = ext3[:, rows - POOL_HALO:, :]
    pos = pos0 + lax.broadcasted_iota(jnp.int32, (n_seq * seq_len, 1), 0) % seq_len
    pool_o = _pool_branch(ext3.reshape(n_seq * rows, d), n_seq, seq_len, pos, wp_ref, ps_ref)
    x1_ref[...] = _merge_project(x_ref[...], on_ref[...], pool_o, sa_ref[...], sp_ref[...], wa_ref, wpb_ref, wo_ref)


def _sample_mix(u, hist, o_n, sig_a, sig_p, x, w_pool, pool_scale, w_attn, w_poolb, w_out, seq_len, pos0):
    n, d = x.shape
    n_seq = n // seq_len
    full = lambda a: pl.BlockSpec(a.shape, lambda i, nd=a.ndim: (0,) * nd)
    args = (u, hist, o_n, sig_a, sig_p, x, w_pool, pool_scale, w_attn, w_poolb, w_out)
    out_shape = [jax.ShapeDtypeStruct(x.shape, F32), jax.ShapeDtypeStruct(hist.shape, F32)]
    return pl.pallas_call(
        functools.partial(_sample_mix_kernel, n_seq=n_seq, seq_len=seq_len, pos0=pos0),
        grid=(1,),
        in_specs=[full(a) for a in args],
        out_specs=[full(s) for s in out_shape],
        out_shape=out_shape,
        compiler_params=_params("arbitrary"),
        name="sample_mix",
    )(*args)


def _conv_taps(up, hist, n_seq, seq_len):
    c = up.shape[-1]
    rows = CONV_HALO + seq_len
    if n_seq == 1:
        ext = jnp.concatenate([hist, up], axis=0)
    else:
        ext = jnp.concatenate([hist.reshape(n_seq, CONV_HALO, c), up.reshape(n_seq, seq_len, c)], axis=1)
        ext = ext.reshape(n_seq * rows, c)

    def tokens(a):
        if n_seq == 1:
            return a[CONV_HALO:]
        return a.reshape(n_seq, rows, c)[:, CONV_HALO:, :].reshape(n_seq * seq_len, c)

    return tokens(pltpu.roll(ext, 1, axis=0)), tokens(pltpu.roll(ext, 2, axis=0))


def _ffn_kernel(x1_ref, nf_ref, wg_ref, wv_ref, cwg_ref, cwv_ref, cbg_ref, cbv_ref, wd_ref, hg_ref, hv_ref,
                y_ref, sg_ref, sv_ref, h_ref, acc_ref, carry_ref, *, n_seq, seq_len, carried):
    j = pl.program_id(2)
    n_j = pl.num_programs(2)

    @pl.when(j == 0)
    def _():
        h_ref[...] = _rms_rows(x1_ref[...], nf_ref[...]).astype(BF16)
        acc_ref[...] = jnp.zeros(acc_ref.shape, F32)

    h = h_ref[...]
    halves = []
    for half, (w_ref, cw_ref, cb_ref, hist_ref, s_ref) in enumerate(
            ((wg_ref, cwg_ref, cbg_ref, hg_ref, sg_ref), (wv_ref, cwv_ref, cbv_ref, hv_ref, sv_ref))):
        up = jnp.dot(h, w_ref[...], preferred_element_type=F32)
        if carried:
            hist = jnp.where(pl.program_id(1) > 0, carry_ref[j, half], 0.0)
            carry_ref[j, half] = up[seq_len - CONV_HALO:]
            s_ref[0] = up[seq_len - CONV_HALO:]
        else:
            hist = hist_ref[...]
            s_ref[...] = up
        prev1, prev2 = _conv_taps(up, hist, n_seq, seq_len)
        cw = cw_ref[...]
        halves.append(cb_ref[...] + prev2 * cw[0:1] + prev1 * cw[1:2] + up * cw[2:3])
    gate, val = halves
    act = (gate * _sigmoid(gate) * val).astype(BF16)
    acc_ref[...] += jnp.dot(act, wd_ref[...], preferred_element_type=F32)

    @pl.when(j == n_j - 1)
    def _():
        y_ref[...] = x1_ref[...] + acc_ref[...]


def _conv_ffn(x1, norm_ffn, w_up, conv_w, conv_b, w_down, hist_gate, hist_val, n_batch, tm, seq_len, carried):
    n, d = x1.shape
    d_ff = w_down.shape[0]
    n_chunks = d_ff // FF_CHUNK
    tiles = n // (n_batch * tm)
    n_seq = 1 if carried else tm // seq_len
    tile_len = tm if carried else seq_len
    row = lambda bi, i, j: (bi * tiles + i, 0)
    gate_col = lambda bi, i, j: (0, j)
    val_col = lambda bi, i, j: (0, n_chunks + j)
    if carried:
        state_shape = jax.ShapeDtypeStruct((n_batch * tiles, CONV_HALO, d_ff), F32)
        state_spec = pl.BlockSpec((1, CONV_HALO, FF_CHUNK), lambda bi, i, j: (bi * tiles + i, 0, j))
        hist_spec_g = pl.BlockSpec((CONV_HALO, FF_CHUNK), lambda bi, i, j: (0, 0))
        hist_spec_v = hist_spec_g
    else:
        state_shape = jax.ShapeDtypeStruct((n, d_ff), F32)
        state_spec = pl.BlockSpec((tm, FF_CHUNK), lambda bi, i, j: (bi * tiles + i, j))
        hist_spec_g = pl.BlockSpec((n_seq * CONV_HALO, FF_CHUNK), lambda bi, i, j: (bi * tiles + i, j))
        hist_spec_v = hist_spec_g
    return pl.pallas_call(
        functools.partial(_ffn_kernel, n_seq=n_seq, seq_len=tile_len, carried=carried),
        grid=(n_batch, tiles, n_chunks),
        in_specs=[pl.BlockSpec((tm, d), row),
                  pl.BlockSpec((1, d), lambda bi, i, j: (0, 0)),
                  pl.BlockSpec((d, FF_CHUNK), gate_col),
                  pl.BlockSpec((d, FF_CHUNK), val_col),
                  pl.BlockSpec((CONV_W, FF_CHUNK), gate_col),
                  pl.BlockSpec((CONV_W, FF_CHUNK), val_col),
                  pl.BlockSpec((1, FF_CHUNK), gate_col),
                  pl.BlockSpec((1, FF_CHUNK), val_col),
                  pl.BlockSpec((FF_CHUNK, d), lambda bi, i, j: (j, 0)),
                  hist_spec_g, hist_spec_v],
        out_specs=[pl.BlockSpec((tm, d), row), state_spec, state_spec],
        out_shape=[jax.ShapeDtypeStruct((n, d), F32), state_shape, state_shape],
        scratch_shapes=[pltpu.VMEM((tm, d), BF16), pltpu.VMEM((tm, d), F32),
                        pltpu.VMEM((n_chunks, 2, CONV_HALO, FF_CHUNK), F32)],
        compiler_params=_params("arbitrary", "arbitrary", "arbitrary"),
        name="conv_ffn",
    )(x1, norm_ffn, w_up, w_up, conv_w, conv_w, conv_b, conv_b, w_down, hist_gate, hist_val)


PROMPT_TM = 512
ATTN_BLK = 512


def kernel(x_prompt, x_sample, cache_k, cache_v, state_pool, state_ffn_conv, page_table, norm_mix, w_in, q_norm, k_norm, lambda_q1, lambda_k1, lambda_q2, lambda_k2, head_norm, w_pool, pool_scale, w_attn_branch, w_pool_branch, w_out, norm_ffn, w_up, conv_w, conv_b, w_down):
    depth = w_in.shape[0]
    b, t, d = x_prompt.shape
    bs, ts, _ = x_sample.shape
    width = N_HEADS * V_DIM
    d_ff = w_down.shape[1]
    past_len = page_table.shape[1] * PAGE_SIZE

    yp, ys = x_prompt, x_sample.reshape(bs * ts, d)
    outs = [[] for _ in range(8)]
    for l in range(depth):
        lam_init = lambda_init_for(l)
        w_in_bf = w_in[l].astype(BF16)
        w_pool_bf = w_pool[l].astype(BF16)
        w_attn_bf = w_attn_branch[l].astype(BF16)
        w_poolb_bf = w_pool_branch[l].astype(BF16)
        w_out_bf = w_out[l].astype(BF16)
        w_up_bf = w_up[l].astype(BF16)
        w_down_bf = w_down[l].astype(BF16)
        nm = norm_mix[l].reshape(1, d)
        nf = norm_ffn[l].reshape(1, d)
        qg = q_norm[l].reshape(1, V_DIM)
        kg = k_norm[l].reshape(1, V_DIM)
        lam_vecs = jnp.stack([lambda_q1[l], lambda_k1[l], lambda_q2[l], lambda_k2[l]])
        ps = pool_scale[l].reshape(1, -1)
        cb = conv_b[l].reshape(1, -1)

        q, kf, kb, vf, vb, u, sa, sp = _in_projection(yp.reshape(b * t, d), nm, w_in_bf, qg, kg, PROMPT_TM)
        r3 = lambda a: a.reshape(b, t, -1)
        o_n = _prompt_attention(lam_vecs, r3(q), r3(kb), r3(vb), head_norm[l], lam_init, ATTN_BLK)
        x1 = _prompt_mix(r3(u), o_n, r3(sa), r3(sp), yp, w_pool_bf, ps, w_attn_bf, w_poolb_bf, w_out_bf, PROMPT_TM)
        no_hist = jnp.zeros((CONV_HALO, FF_CHUNK), F32)
        y2d, fg, fv = _conv_ffn(x1.reshape(b * t, d), nf, w_up_bf, conv_w[l], cb, w_down_bf, no_hist, no_hist,
                                n_batch=b, tm=PROMPT_TM, seq_len=t, carried=True)
        yp = y2d.reshape(b, t, d)
        outs[0].append(kf.reshape(b, t, N_HEADS, V_DIM))
        outs[1].append(vf.reshape(b, t, N_HEADS, V_DIM))
        outs[2].append(r3(u)[:, t - POOL_BUF:, :])
        last_tile = lambda a: a.reshape(b, -1, CONV_HALO, d_ff)[:, -1, CONV_HALO - (CONV_W - 1):, :]
        outs[3].append(jnp.concatenate([last_tile(fg), last_tile(fv)], axis=-1))

        n_s = bs * ts
        q, kf, kb, vf, vb, u, sa, sp = _in_projection(ys, nm, w_in_bf, qg, kg, n_s)
        s3 = lambda a: a.reshape(bs, ts, -1)
        o_n = _decode_attention(page_table, lam_vecs, s3(q),
                                kf.reshape(bs, ts * N_HEADS, V_DIM), vf.reshape(bs, ts * N_HEADS, V_DIM),
                                cache_k[l], cache_v[l], head_norm[l], lam_init)
        pool_hist = jnp.pad(state_pool[l], ((0, 0), (POOL_HALO - POOL_BUF, 0), (0, 0)))
        x1, pool_state = _sample_mix(u, pool_hist, o_n.reshape(n_s, width), sa, sp, ys, w_pool_bf, ps,
                                     w_attn_bf, w_poolb_bf, w_out_bf, ts, past_len)
        conv_hist = jnp.pad(state_ffn_conv[l], ((0, 0), (CONV_HALO - (CONV_W - 1), 0), (0, 0)))
        conv_hist = conv_hist.reshape(bs * CONV_HALO, 2 * d_ff)
        ys, fg, fv = _conv_ffn(x1, nf, w_up_bf, conv_w[l], cb, w_down_bf,
                               conv_hist[:, :d_ff], conv_hist[:, d_ff:],
                               n_batch=1, tm=n_s, seq_len=ts, carried=False)
        outs[4].append(kf.reshape(bs, ts, N_HEADS, V_DIM))
        outs[5].append(vf.reshape(bs, ts, N_HEADS, V_DIM))
        outs[6].append(pool_state[:, POOL_HALO - POOL_BUF:, :])
        ffn_state = jnp.concatenate([fg, fv], axis=-1).reshape(bs, ts, 2 * d_ff)
        outs[7].append(ffn_state[:, ts - (CONV_W - 1):, :])

    return (yp, ys.reshape(bs, ts, d)) + tuple(jnp.stack(o) for o in outs)
```

```python
import functools
import math

import jax
import jax.numpy as jnp
from jax import lax
from jax.experimental import pallas as pl
from jax.experimental.pallas import tpu as pltpu

F32 = jnp.float32
BF16 = jnp.bfloat16

N_HEADS = 8
HEAD_DIM = 64
V_DIM = 2 * HEAD_DIM
POOL_WINDOWS = (2, 4, 8, 16)
POOL_GROUP = 256
POOL_BUF = max(POOL_WINDOWS) - 1
POOL_HALO = 16
CONV_W = 3
CONV_HALO = 8
PAGE_SIZE = 128
EPS = 1e-6
NEG = -1e30
Q_SCALE = HEAD_DIM ** -0.5 * math.log2(math.e)
FF_CHUNK = 256
IN_SECTION = 1024
VMEM_LIMIT = 56 * 1024 * 1024


def lambda_init_for(layer):
    return 0.8 - 0.6 * math.exp(-0.3 * layer)


def _params(*sem):
    return pltpu.CompilerParams(dimension_semantics=sem, vmem_limit_bytes=VMEM_LIMIT)


def _sigmoid(x):
    return 1.0 / (1.0 + jnp.exp(-x))


def _rms_rows(x, g):
    return x * lax.rsqrt(jnp.mean(x * x, axis=-1, keepdims=True) + EPS) * g


def _lambda_value(lam_ref, lam_init):
    lv = lam_ref[...]
    s1 = jnp.sum(lv[0:1] * lv[1:2], axis=-1, keepdims=True)
    s2 = jnp.sum(lv[2:3] * lv[3:4], axis=-1, keepdims=True)
    return jnp.exp(s1) - jnp.exp(s2) + lam_init


def _inproj_kernel(x_ref, nm_ref, w_ref, qg_ref, kg_ref,
                   q_ref, kf_ref, kb_ref, vf_ref, vb_ref, u_ref, sa_ref, sp_ref, h_ref):
    j = pl.program_id(1)

    @pl.when(j == 0)
    def _():
        h_ref[...] = _rms_rows(x_ref[...], nm_ref[...]).astype(BF16)

    y = jnp.dot(h_ref[...], w_ref[...], preferred_element_type=F32)

    def group_norm(g_ref):
        r = lax.broadcasted_iota(jnp.int32, (V_DIM, V_DIM), 0) // HEAD_DIM
        c = lax.broadcasted_iota(jnp.int32, (V_DIM, V_DIM), 1) // HEAD_DIM
        ones_bd = (r == c).astype(BF16)
        parts = []
        for h in range(N_HEADS):
            yh = y[:, h * V_DIM:(h + 1) * V_DIM]
            ss = jnp.dot((yh * yh).astype(BF16), ones_bd, preferred_element_type=F32)
            parts.append(yh * lax.rsqrt(ss * (1.0 / HEAD_DIM) + EPS) * g_ref[...])
        return jnp.concatenate(parts, axis=-1)

    @pl.when(j == 0)
    def _():
        q_ref[...] = (group_norm(qg_ref) * Q_SCALE).astype(BF16)

    def store_heads(dst_ref, val):
        for h in range(N_HEADS):
            dst_ref[:, h, :] = val[:, h * V_DIM:(h + 1) * V_DIM]

    @pl.when(j == 1)
    def _():
        kn = group_norm(kg_ref)
        store_heads(kf_ref, kn)
        kb_ref[...] = kn.astype(BF16)

    @pl.when(j == 2)
    def _():
        store_heads(vf_ref, y)
        vb_ref[...] = y.T.astype(BF16)

    @pl.when(j == 3)
    def _():
        u_ref[...] = y

    @pl.when(j == 4)
    def _():
        sa_ref[...] = _sigmoid(y)

    @pl.when(j == 5)
    def _():
        sp_ref[...] = _sigmoid(y)


def _in_projection(x2d, norm_mix, w_in_bf, q_gain, k_gain, tm):
    n, d = x2d.shape
    n_sections = w_in_bf.shape[1] // IN_SECTION
    row = lambda i, j: (i, 0)
    fixed = lambda i, j: (0, 0)
    tile = pl.BlockSpec((tm, IN_SECTION), row)
    heads_tile = pl.BlockSpec((tm, N_HEADS, V_DIM), lambda i, j: (i, 0, 0))
    flat = lambda dt: jax.ShapeDtypeStruct((n, IN_SECTION), dt)
    heads = jax.ShapeDtypeStruct((n, N_HEADS, V_DIM), F32)
    v_t_tile = pl.BlockSpec((IN_SECTION, tm), lambda i, j: (0, i))
    v_t = jax.ShapeDtypeStruct((IN_SECTION, n), BF16)
    out_specs = [tile, heads_tile, tile, heads_tile, v_t_tile, tile, tile, tile]
    out_shape = [flat(BF16), heads, flat(BF16), heads, v_t, flat(F32), flat(F32), flat(F32)]
    return pl.pallas_call(
        _inproj_kernel,
        grid=(n // tm, n_sections),
        in_specs=[pl.BlockSpec((tm, d), row),
                  pl.BlockSpec((1, d), fixed),
                  pl.BlockSpec((d, IN_SECTION), lambda i, j: (0, j)),
                  pl.BlockSpec((1, V_DIM), fixed),
                  pl.BlockSpec((1, V_DIM), fixed)],
        out_specs=out_specs,
        out_shape=out_shape,
        scratch_shapes=[pltpu.VMEM((tm, d), BF16)],
        compiler_params=_params("arbitrary", "arbitrary"),
        name="in_projection",
    )(x2d, norm_mix, w_in_bf, q_gain, k_gain)


def _prompt_attn_kernel(lam_ref, q_ref, k_ref, vt_ref, hn_ref, o_ref, m_ref, l_ref, acc_ref, st0_ref, st1_ref,
                        *, blk_q, blk_k, lam_init):
    i = pl.program_id(2)
    q = q_ref[0]
    lane = lax.broadcasted_iota(jnp.int32, q.shape, 1)
    zero = jnp.zeros_like(q)
    q_comp = (jnp.where(lane < HEAD_DIM, q, zero), jnp.where(lane >= HEAD_DIM, q, zero))

    m_ref[...] = jnp.full(m_ref.shape, NEG, F32)
    l_ref[...] = jnp.zeros(l_ref.shape, F32)
    acc_ref[...] = jnp.zeros(acc_ref.shape, F32)

    def scores(j, st_ref):
        start = pl.multiple_of(j * blk_k, blk_k)
        k = k_ref[0, pl.ds(start, blk_k), :]
        for c in range(2):
            st_ref[c] = lax.dot_general(k, q_comp[c], (((1,), (1,)), ((), ())), preferred_element_type=F32)

    def update(j, st_ref, masked):
        start = pl.multiple_of(j * blk_k, blk_k)
        vt = vt_ref[:, pl.ds(start, blk_k)]
        for c in range(2):
            st = st_ref[c]
            if masked:
                k_pos = start + lax.broadcasted_iota(jnp.int32, st.shape, 0)
                q_pos = i * blk_q + lax.broadcasted_iota(jnp.int32, st.shape, 1)
                st = jnp.where(k_pos <= q_pos, st, NEG)
            m_old = m_ref[c]
            m_new = jnp.maximum(m_old, jnp.max(st, axis=0, keepdims=True))
            pt = jnp.exp2(st - m_new)
            alpha = jnp.exp2(m_old - m_new)
            l_ref[c] = alpha * l_ref[c] + jnp.sum(pt, axis=0, keepdims=True)
            acc_ref[c] = alpha * acc_ref[c] + jnp.dot(vt, pt.astype(BF16), preferred_element_type=F32)
            m_ref[c] = m_new

    def body(jj, carry):
        scores(2 * jj + 1, st1_ref)
        update(2 * jj, st0_ref, False)
        scores(2 * jj + 2, st0_ref)
        update(2 * jj + 1, st1_ref, False)
        return carry

    assert blk_q == 2 * blk_k
    scores(0, st0_ref)
    lax.fori_loop(0, i, body, 0)
    scores(2 * i + 1, st1_ref)
    update(2 * i, st0_ref, True)
    update(2 * i + 1, st1_ref, True)

    lam = _lambda_value(lam_ref, lam_init)
    ot = acc_ref[0] / l_ref[0] - lam * (acc_ref[1] / l_ref[1])
    ot = ot * lax.rsqrt(jnp.mean(ot * ot, axis=0, keepdims=True) + EPS) * hn_ref[0] * (1.0 - lam_init)
    o_ref[0] = ot.T.astype(BF16)


def _prompt_attention(lam_vecs, q, k, v_t, head_norm, lam_init, blk_q, blk_k):
    b, t, _ = q.shape
    q_spec = pl.BlockSpec((1, blk_q, V_DIM), lambda bi, h, i: (bi, i, h))
    return pl.pallas_call(
        functools.partial(_prompt_attn_kernel, blk_q=blk_q, blk_k=blk_k, lam_init=lam_init),
        grid=(b, N_HEADS, t // blk_q),
        in_specs=[pl.BlockSpec(lam_vecs.shape, lambda bi, h, i: (0, 0)),
                  q_spec,
                  pl.BlockSpec((1, t, V_DIM), lambda bi, h, i: (bi, 0, h)),
                  pl.BlockSpec((V_DIM, t), lambda bi, h, i: (h, bi)),
                  pl.BlockSpec((1, V_DIM, 1), lambda bi, h, i: (h, 0, 0))],
        out_specs=q_spec,
        out_shape=jax.ShapeDtypeStruct(q.shape, BF16),
        scratch_shapes=[pltpu.VMEM((2, 1, blk_q), F32), pltpu.VMEM((2, 1, blk_q), F32),
                        pltpu.VMEM((2, V_DIM, blk_q), F32),
                        pltpu.VMEM((2, blk_k, blk_q), F32), pltpu.VMEM((2, blk_k, blk_q), F32)],
        compiler_params=_params("arbitrary", "arbitrary", "arbitrary"),
        name="prompt_attention",
    )(lam_vecs, q, k, v_t, head_norm.reshape(N_HEADS, V_DIM, 1))


def _decode_attn_kernel(pt_ref, lam_ref, q_ref, kn_ref, vn_ref, *refs, n_new, n_pg, lam_init):
    del pt_ref
    ck_refs, cv_refs = refs[:n_pg], refs[n_pg:2 * n_pg]
    hn_ref, o_ref, a_ref, m_ref, l_ref, acc_ref = refs[2 * n_pg:]
    p_idx = pl.program_id(1)
    n_cols = 2 * N_HEADS * n_new
    cols_per_head = 2 * n_new

    def head_of(shape, row_axis, col_axis):
        r = lax.broadcasted_iota(jnp.int32, shape, row_axis)
        c = lax.broadcasted_iota(jnp.int32, shape, col_axis)
        return r, c, c // cols_per_head

    def update(blocks, causal):
        h_idx, c_idx, c_head = head_of((N_HEADS, n_cols), 0, 1)
        own_head = h_idx == c_head
        scores, valids = [], []
        for k2d, _ in blocks:
            n_tok = k2d.shape[0] // N_HEADS
            s = lax.dot_general(k2d, a_ref[...], (((1,), (1,)), ((), ())), preferred_element_type=F32)
            s3 = s.reshape(n_tok, N_HEADS, n_cols)
            valid = own_head[None]
            if causal:
                t_idx = lax.broadcasted_iota(jnp.int32, s3.shape, 0)
                q_idx = lax.broadcasted_iota(jnp.int32, s3.shape, 2) % n_new
                valid = valid & (t_idx <= q_idx)
            scores.append(s3)
            valids.append(valid)
        m_old = m_ref[...]
        m_new = m_old
        for s3, valid in zip(scores, valids):
            m_new = jnp.maximum(m_new, jnp.max(jnp.where(valid, s3, NEG), axis=0))
        alpha = jnp.exp2(m_old - m_new)
        l_new = alpha * l_ref[...]
        pv = None
        for (_, v2d), s3, valid in zip(blocks, scores, valids):
            p3 = jnp.where(valid, jnp.exp2(s3 - m_new[None]), 0.0)
            l_new = l_new + jnp.sum(p3, axis=0)
            part = lax.dot_general(v2d, p3.reshape(v2d.shape[0], n_cols), (((0,), (0,)), ((), ())),
                                   preferred_element_type=F32)
            pv = part if pv is None else pv + part
        l_ref[...] = l_new
        m_ref[...] = m_new
        alpha_row = jnp.sum(jnp.where(own_head, alpha, 0.0), axis=0, keepdims=True)
        acc_ref[...] = acc_ref[...] * alpha_row + pv

    @pl.when(p_idx == 0)
    def _():
        q = q_ref[0].astype(F32)
        lane = lax.broadcasted_iota(jnp.int32, (n_new, V_DIM), 1)
        rows = []
        for h in range(N_HEADS):
            qh = q[:, h * V_DIM:(h + 1) * V_DIM]
            rows += [jnp.where(lane < HEAD_DIM, qh, 0.0), jnp.where(lane >= HEAD_DIM, qh, 0.0)]
        a_ref[...] = jnp.concatenate(rows, axis=0)
        m_ref[...] = jnp.full(m_ref.shape, NEG, F32)
        l_ref[...] = jnp.zeros(l_ref.shape, F32)
        acc_ref[...] = jnp.zeros(acc_ref.shape, F32)
        update([(kn_ref[0], vn_ref[0])], True)

    rows = PAGE_SIZE * N_HEADS
    update([(ck[0].reshape(rows, V_DIM), cv[0].reshape(rows, V_DIM)) for ck, cv in zip(ck_refs, cv_refs)], False)

    @pl.when(p_idx == pl.num_programs(1) - 1)
    def _():
        lam = _lambda_value(lam_ref, lam_init)
        h_idx, c_idx, c_head = head_of((N_HEADS, n_cols), 0, 1)
        l_row = jnp.sum(jnp.where(h_idx == c_head, l_ref[...], 0.0), axis=0, keepdims=True)
        on = (acc_ref[...] / l_row).T
        parts = []
        for h in range(N_HEADS):
            blk = on[h * cols_per_head:(h + 1) * cols_per_head]
            o = blk[:n_new] - lam * blk[n_new:]
            parts.append(_rms_rows(o, hn_ref[:, h * V_DIM:(h + 1) * V_DIM]) * (1.0 - lam_init))
        o_ref[0] = jnp.concatenate(parts, axis=-1).astype(BF16)


def _decode_attention(page_table, lam_vecs, q, k_new, v_new, cache_k, cache_v, head_norm, lam_init):
    b, n_new, width = q.shape
    n_pages = page_table.shape[1]
    n_cols = 2 * N_HEADS * n_new
    seq_spec = pl.BlockSpec((1, n_new, width), lambda bi, p, pt: (bi, 0, 0))
    new_spec = pl.BlockSpec((1, n_new * N_HEADS, V_DIM), lambda bi, p, pt: (bi, 0, 0))
    n_pg = DECODE_PAGES_PER_STEP
    page_specs = [pl.BlockSpec((1, PAGE_SIZE, N_HEADS, V_DIM),
                               lambda bi, p, pt, s=s: (pt[bi, p * n_pg + s], 0, 0, 0)) for s in range(n_pg)]
    grid_spec = pltpu.PrefetchScalarGridSpec(
        num_scalar_prefetch=1,
        grid=(b, n_pages // n_pg),
        in_specs=[pl.BlockSpec(lam_vecs.shape, lambda bi, p, pt: (0, 0)),
                  seq_spec, new_spec, new_spec, *page_specs, *page_specs,
                  pl.BlockSpec((1, width), lambda bi, p, pt: (0, 0))],
        out_specs=seq_spec,
        scratch_shapes=[pltpu.VMEM((n_cols, V_DIM), F32), pltpu.VMEM((N_HEADS, n_cols), F32),
                        pltpu.VMEM((N_HEADS, n_cols), F32), pltpu.VMEM((V_DIM, n_cols), F32)],
    )
    return pl.pallas_call(
        functools.partial(_decode_attn_kernel, n_new=n_new, n_pg=n_pg, lam_init=lam_init),
        grid_spec=grid_spec,
        out_shape=jax.ShapeDtypeStruct(q.shape, BF16),
        compiler_params=_params("arbitrary", "arbitrary"),
        name="decode_attention",
    )(page_table, lam_vecs, q, k_new, v_new, *([cache_k] * n_pg), *([cache_v] * n_pg),
      head_norm.reshape(1, width))


def _pool_branch(ext, n_seq, seq_len, pos, wp_ref, ps_ref):
    rows = POOL_HALO + seq_len

    def tokens(a):
        if n_seq == 1:
            return a[POOL_HALO:]
        return a.reshape(n_seq, rows, a.shape[-1])[:, POOL_HALO:, :].reshape(n_seq * seq_len, a.shape[-1])

    cur = ext
    outs = []
    for g, w in enumerate(POOL_WINDOWS):
        cur = cur + pltpu.roll(cur, w // 2, axis=0)
        c0 = g * POOL_GROUP
        win = tokens(cur[:, :POOL_GROUP])
        tok = tokens(ext[:, c0:c0 + POOL_GROUP])
        cnt = jnp.minimum(w, pos + 1).astype(F32)
        d = (win / cnt - tok).astype(BF16)
        outs.append(jnp.dot(d, wp_ref[g], preferred_element_type=F32) * ps_ref[:, c0:c0 + POOL_GROUP])
        cur = cur[:, POOL_GROUP:]
    return jnp.concatenate(outs, axis=-1)


def _merge_project(x, o_n, pool_o, sig_a, sig_p, wa_ref, wpb_ref, wo_ref):
    attn = jnp.dot(o_n, wa_ref[...], preferred_element_type=F32)
    pool = jnp.dot(pool_o.astype(BF16), wpb_ref[...], preferred_element_type=F32)
    merged = sig_a * attn + sig_p * pool
    return x + jnp.dot(merged.astype(BF16), wo_ref[...], preferred_element_type=F32)


def _prompt_mix_kernel(u_ref, halo_ref, on_ref, sa_ref, sp_ref, x_ref, wp_ref, ps_ref, wa_ref, wpb_ref, wo_ref,
                       x1_ref, *, tm):
    i = pl.program_id(1)
    halo = jnp.where(i > 0, halo_ref[0], 0.0)
    ext = jnp.concatenate([halo, u_ref[0]], axis=0)
    pos = i * tm + lax.broadcasted_iota(jnp.int32, (tm, 1), 0)
    pool_o = _pool_branch(ext, 1, tm, pos, wp_ref, ps_ref)
    x1_ref[0] = _merge_project(x_ref[0], on_ref[0], pool_o, sa_ref[0], sp_ref[0], wa_ref, wpb_ref, wo_ref)


def _prompt_mix(u, o_n, sig_a, sig_p, x, w_pool, pool_scale, w_attn, w_poolb, w_out, tm):
    b, t, d = x.shape
    tile = lambda bi, i: (bi, i, 0)
    fixed2 = lambda bi, i: (0, 0)
    halo_blocks = tm // POOL_HALO
    tile_spec = pl.BlockSpec((1, tm, d), tile)
    return pl.pallas_call(
        functools.partial(_prompt_mix_kernel, tm=tm),
        grid=(b, t // tm),
        in_specs=[tile_spec,
                  pl.BlockSpec((1, POOL_HALO, d), lambda bi, i: (bi, jnp.maximum(i * halo_blocks - 1, 0), 0)),
                  tile_spec, tile_spec, tile_spec, tile_spec,
                  pl.BlockSpec(w_pool.shape, lambda bi, i: (0, 0, 0)),
                  pl.BlockSpec((1, d), fixed2),
                  pl.BlockSpec(w_attn.shape, fixed2),
                  pl.BlockSpec(w_poolb.shape, fixed2),
                  pl.BlockSpec(w_out.shape, fixed2)],
        out_specs=tile_spec,
        out_shape=jax.ShapeDtypeStruct(x.shape, F32),
        compiler_params=_params("arbitrary", "arbitrary"),
        name="prompt_mix",
    )(u, u, o_n, sig_a, sig_p, x, w_pool, pool_scale, w_attn, w_poolb, w_out)


def _sample_mix_kernel(u_ref, hist_ref, on_ref, sa_ref, sp_ref, x_ref, wp_ref, ps_ref, wa_ref, wpb_ref, wo_ref,
                       x1_ref, state_ref, *, n_seq, seq_len, pos0):
    d = u_ref.shape[-1]
    rows = POOL_HALO + seq_len
    ext3 = jnp.concatenate([hist_ref[...], u_ref[...].reshape(n_seq, seq_len, d)], axis=1)
    state_ref[...] = ext3[:, rows - POOL_HALO:, :]
    pos = pos0 + lax.broadcasted_iota(jnp.int32, (n_seq * seq_len, 1), 0) % seq_len
    pool_o = _pool_branch(ext3.reshape(n_seq * rows, d), n_seq, seq_len, pos, wp_ref, ps_ref)
    x1_ref[...] = _merge_project(x_ref[...], on_ref[...], pool_o, sa_ref[...], sp_ref[...], wa_ref, wpb_ref, wo_ref)


def _sample_mix(u, hist, o_n, sig_a, sig_p, x, w_pool, pool_scale, w_attn, w_poolb, w_out, seq_len, pos0):
    n, d = x.shape
    n_seq = n // seq_len
    full = lambda a: pl.BlockSpec(a.shape, lambda i, nd=a.ndim: (0,) * nd)
    args = (u, hist, o_n, sig_a, sig_p, x, w_pool, pool_scale, w_attn, w_poolb, w_out)
    out_shape = [jax.ShapeDtypeStruct(x.shape, F32), jax.ShapeDtypeStruct(hist.shape, F32)]
    return pl.pallas_call(
        functools.partial(_sample_mix_kernel, n_seq=n_seq, seq_len=seq_len, pos0=pos0),
        grid=(1,),
        in_specs=[full(a) for a in args],
        out_specs=[full(s) for s in out_shape],
        out_shape=out_shape,
        compiler_params=_params("arbitrary"),
        name="sample_mix",
    )(*args)


def _conv_taps(up, hist, n_seq, seq_len):
    c = up.shape[-1]
    rows = CONV_HALO + seq_len
    if n_seq == 1:
        ext = jnp.concatenate([hist, up], axis=0)
    else:
        ext = jnp.concatenate([hist.reshape(n_seq, CONV_HALO, c), up.reshape(n_seq, seq_len, c)], axis=1)
        ext = ext.reshape(n_seq * rows, c)

    def tokens(a):
        if n_seq == 1:
            return a[CONV_HALO:]
        return a.reshape(n_seq, rows, c)[:, CONV_HALO:, :].reshape(n_seq * seq_len, c)

    return tokens(pltpu.roll(ext, 1, axis=0)), tokens(pltpu.roll(ext, 2, axis=0))


def _ffn_kernel(x1_ref, nf_ref, wg_ref, wv_ref, cwg_ref, cwv_ref, cbg_ref, cbv_ref, wd_ref, hg_ref, hv_ref,
                y_ref, sg_ref, sv_ref, h_ref, acc_ref, carry_ref, *, n_seq, seq_len, carried):
    j = pl.program_id(2)
    n_j = pl.num_programs(2)

    @pl.when(j == 0)
    def _():
        h_ref[...] = _rms_rows(x1_ref[...], nf_ref[...]).astype(BF16)
        acc_ref[...] = jnp.zeros(acc_ref.shape, F32)

    h = h_ref[...]
    halves = []
    for half, (w_ref, cw_ref, cb_ref, hist_ref, s_ref) in enumerate(
            ((wg_ref, cwg_ref, cbg_ref, hg_ref, sg_ref), (wv_ref, cwv_ref, cbv_ref, hv_ref, sv_ref))):
        up = jnp.dot(h, w_ref[...], preferred_element_type=F32)
        if carried:
            hist = jnp.where(pl.program_id(1) > 0, carry_ref[j, half], 0.0)
            carry_ref[j, half] = up[seq_len - CONV_HALO:]
            s_ref[0] = up[seq_len - CONV_HALO:]
        else:
            hist = hist_ref[...]
            s_ref[...] = up
        prev1, prev2 = _conv_taps(up, hist, n_seq, seq_len)
        cw = cw_ref[...]
        halves.append(cb_ref[...] + prev2 * cw[0:1] + prev1 * cw[1:2] + up * cw[2:3])
    gate, val = halves
    act = (gate * _sigmoid(gate) * val).astype(BF16)
    acc_ref[...] += jnp.dot(act, wd_ref[...], preferred_element_type=F32)

    @pl.when(j == n_j - 1)
    def _():
        y_ref[...] = x1_ref[...] + acc_ref[...]


def _conv_ffn(x1, norm_ffn, w_up, conv_w, conv_b, w_down, hist_gate, hist_val, n_batch, tm, seq_len, carried):
    n, d = x1.shape
    d_ff = w_down.shape[0]
    n_chunks = d_ff // FF_CHUNK
    tiles = n // (n_batch * tm)
    n_seq = 1 if carried else tm // seq_len
    tile_len = tm if carried else seq_len
    row = lambda bi, i, j: (bi * tiles + i, 0)
    gate_col = lambda bi, i, j: (0, j)
    val_col = lambda bi, i, j: (0, n_chunks + j)
    if carried:
        state_shape = jax.ShapeDtypeStruct((n_batch * tiles, CONV_HALO, d_ff), F32)
        state_spec = pl.BlockSpec((1, CONV_HALO, FF_CHUNK), lambda bi, i, j: (bi * tiles + i, 0, j))
        hist_spec_g = pl.BlockSpec((CONV_HALO, FF_CHUNK), lambda bi, i, j: (0, 0))
        hist_spec_v = hist_spec_g
    else:
        state_shape = jax.ShapeDtypeStruct((n, d_ff), F32)
        state_spec = pl.BlockSpec((tm, FF_CHUNK), lambda bi, i, j: (bi * tiles + i, j))
        hist_spec_g = pl.BlockSpec((n_seq * CONV_HALO, FF_CHUNK), lambda bi, i, j: (bi * tiles + i, j))
        hist_spec_v = hist_spec_g
    return pl.pallas_call(
        functools.partial(_ffn_kernel, n_seq=n_seq, seq_len=tile_len, carried=carried),
        grid=(n_batch, tiles, n_chunks),
        in_specs=[pl.BlockSpec((tm, d), row),
                  pl.BlockSpec((1, d), lambda bi, i, j: (0, 0)),
                  pl.BlockSpec((d, FF_CHUNK), gate_col),
                  pl.BlockSpec((d, FF_CHUNK), val_col),
                  pl.BlockSpec((CONV_W, FF_CHUNK), gate_col),
                  pl.BlockSpec((CONV_W, FF_CHUNK), val_col),
                  pl.BlockSpec((1, FF_CHUNK), gate_col),
                  pl.BlockSpec((1, FF_CHUNK), val_col),
                  pl.BlockSpec((FF_CHUNK, d), lambda bi, i, j: (j, 0)),
                  hist_spec_g, hist_spec_v],
        out_specs=[pl.BlockSpec((tm, d), row), state_spec, state_spec],
        out_shape=[jax.ShapeDtypeStruct((n, d), F32), state_shape, state_shape],
        scratch_shapes=[pltpu.VMEM((tm, d), BF16), pltpu.VMEM((tm, d), F32),
                        pltpu.VMEM((n_chunks, 2, CONV_HALO, FF_CHUNK), F32)],
        compiler_params=_params("arbitrary", "arbitrary", "arbitrary"),
        name="conv_ffn",
    )(x1, norm_ffn, w_up, w_up, conv_w, conv_w, conv_b, conv_b, w_down, hist_gate, hist_val)


PROMPT_TM = 512
DECODE_PAGES_PER_STEP = 4
ATTN_BLK_Q = 512
ATTN_BLK_K = 256


def kernel(x_prompt, x_sample, cache_k, cache_v, state_pool, state_ffn_conv, page_table, norm_mix, w_in, q_norm, k_norm, lambda_q1, lambda_k1, lambda_q2, lambda_k2, head_norm, w_pool, pool_scale, w_attn_branch, w_pool_branch, w_out, norm_ffn, w_up, conv_w, conv_b, w_down):
    depth = w_in.shape[0]
    b, t, d = x_prompt.shape
    bs, ts, _ = x_sample.shape
    width = N_HEADS * V_DIM
    d_ff = w_down.shape[1]
    past_len = page_table.shape[1] * PAGE_SIZE

    yp, ys = x_prompt, x_sample.reshape(bs * ts, d)
    outs = [[] for _ in range(8)]
    for l in range(depth):
        lam_init = lambda_init_for(l)
        w_in_bf = w_in[l].astype(BF16)
        w_pool_bf = w_pool[l].astype(BF16)
        w_attn_bf = w_attn_branch[l].astype(BF16)
        w_poolb_bf = w_pool_branch[l].astype(BF16)
        w_out_bf = w_out[l].astype(BF16)
        w_up_bf = w_up[l].astype(BF16)
        w_down_bf = w_down[l].astype(BF16)
        nm = norm_mix[l].reshape(1, d)
        nf = norm_ffn[l].reshape(1, d)
        qg = q_norm[l].reshape(1, V_DIM)
        kg = k_norm[l].reshape(1, V_DIM)
        lam_vecs = jnp.stack([lambda_q1[l], lambda_k1[l], lambda_q2[l], lambda_k2[l]])
        ps = pool_scale[l].reshape(1, -1)
        cb = conv_b[l].reshape(1, -1)

        q, kf, kb, vf, vb, u, sa, sp = _in_projection(yp.reshape(b * t, d), nm, w_in_bf, qg, kg, PROMPT_TM)
        r3 = lambda a: a.reshape(b, t, -1)
        o_n = _prompt_attention(lam_vecs, r3(q), r3(kb), vb, head_norm[l], lam_init, ATTN_BLK_Q, ATTN_BLK_K)
        x1 = _prompt_mix(r3(u), o_n, r3(sa), r3(sp), yp, w_pool_bf, ps, w_attn_bf, w_poolb_bf, w_out_bf, PROMPT_TM)
        no_hist = jnp.zeros((CONV_HALO, FF_CHUNK), F32)
        y2d, fg, fv = _conv_ffn(x1.reshape(b * t, d), nf, w_up_bf, conv_w[l], cb, w_down_bf, no_hist, no_hist,
                                n_batch=b, tm=PROMPT_TM, seq_len=t, carried=True)
        yp = y2d.reshape(b, t, d)
        outs[0].append(kf.reshape(b, t, N_HEADS, V_DIM))
        outs[1].append(vf.reshape(b, t, N_HEADS, V_DIM))
        outs[2].append(r3(u)[:, t - POOL_BUF:, :])
        last_tile = lambda a: a.reshape(b, -1, CONV_HALO, d_ff)[:, -1, CONV_HALO - (CONV_W - 1):, :]
        outs[3].append(jnp.concatenate([last_tile(fg), last_tile(fv)], axis=-1))

        n_s = bs * ts
        q, kf, kb, vf, vb, u, sa, sp = _in_projection(ys, nm, w_in_bf, qg, kg, n_s)
        s3 = lambda a: a.reshape(bs, ts, -1)
        o_n = _decode_attention(page_table, lam_vecs, s3(q),
                                kf.reshape(bs, ts * N_HEADS, V_DIM), vf.reshape(bs, ts * N_HEADS, V_DIM),
                                cache_k[l], cache_v[l], head_norm[l], lam_init)
        pool_hist = jnp.pad(state_pool[l], ((0, 0), (POOL_HALO - POOL_BUF, 0), (0, 0)))
        x1, pool_state = _sample_mix(u, pool_hist, o_n.reshape(n_s, width), sa, sp, ys, w_pool_bf, ps,
                                     w_attn_bf, w_poolb_bf, w_out_bf, ts, past_len)
        conv_hist = jnp.pad(state_ffn_conv[l], ((0, 0), (CONV_HALO - (CONV_W - 1), 0), (0, 0)))
        conv_hist = conv_hist.reshape(bs * CONV_HALO, 2 * d_ff)
        ys, fg, fv = _conv_ffn(x1, nf, w_up_bf, conv_w[l], cb, w_down_bf,
                               conv_hist[:, :d_ff], conv_hist[:, d_ff:],
                               n_batch=1, tm=n_s, seq_len=ts, carried=False)
        outs[4].append(kf.reshape(bs, ts, N_HEADS, V_DIM))
        outs[5].append(vf.reshape(bs, ts, N_HEADS, V_DIM))
        outs[6].append(pool_state[:, POOL_HALO - POOL_BUF:, :])
        ffn_state = jnp.concatenate([fg, fv], axis=-1).reshape(bs, ts, 2 * d_ff)
        outs[7].append(ffn_state[:, ts - (CONV_W - 1):, :])

    return (yp, ys.reshape(bs, ts, d)) + tuple(jnp.stack(o) for o in outs)
```

```python
import functools
import math

import jax
import jax.numpy as jnp
from jax import lax
from jax.experimental import pallas as pl
from jax.experimental.pallas import tpu as pltpu

F32 = jnp.float32
BF16 = jnp.bfloat16

N_HEADS = 8
HEAD_DIM = 64
V_DIM = 2 * HEAD_DIM
POOL_WINDOWS = (2, 4, 8, 16)
POOL_GROUP = 256
POOL_BUF = max(POOL_WINDOWS) - 1
POOL_HALO = 16
CONV_W = 3
CONV_HALO = 8
PAGE_SIZE = 128
SUM_ROWS = 16
EPS = 1e-6
NEG = -1e30
Q_SCALE = HEAD_DIM ** -0.5 * math.log2(math.e)
FF_CHUNK = 256
IN_SECTION = 1024
VMEM_LIMIT = 56 * 1024 * 1024


def lambda_init_for(layer):
    return 0.8 - 0.6 * math.exp(-0.3 * layer)


def _params(*sem):
    return pltpu.CompilerParams(dimension_semantics=sem, vmem_limit_bytes=VMEM_LIMIT)


def _sigmoid(x):
    return 1.0 / (1.0 + jnp.exp(-x))


def _rms_rows(x, g):
    return x * lax.rsqrt(jnp.mean(x * x, axis=-1, keepdims=True) + EPS) * g


def _lambda_value(lam_ref, lam_init):
    lv = lam_ref[...]
    s1 = jnp.sum(lv[0:1] * lv[1:2], axis=-1, keepdims=True)
    s2 = jnp.sum(lv[2:3] * lv[3:4], axis=-1, keepdims=True)
    return jnp.exp(s1) - jnp.exp(s2) + lam_init


def _inproj_kernel(x_ref, nm_ref, w_ref, qg_ref, kg_ref,
                   q_ref, kf_ref, kb_ref, vf_ref, vb_ref, u_ref, sa_ref, sp_ref, h_ref, *, n_split):
    j = pl.program_id(1)
    part = x_ref.shape[0] // n_split

    def section(epilogue):
        for r in range(n_split):
            rows = slice(r * part, (r + 1) * part)
            epilogue(rows, jnp.dot(h_ref[rows, :], w_ref[...], preferred_element_type=F32))

    def group_norm(y, g_ref):
        r = lax.broadcasted_iota(jnp.int32, (V_DIM, V_DIM), 0) // HEAD_DIM
        c = lax.broadcasted_iota(jnp.int32, (V_DIM, V_DIM), 1) // HEAD_DIM
        ones_bd = (r == c).astype(BF16)
        parts = []
        for h in range(N_HEADS):
            yh = y[:, h * V_DIM:(h + 1) * V_DIM]
            ss = jnp.dot((yh * yh).astype(BF16), ones_bd, preferred_element_type=F32)
            parts.append(yh * lax.rsqrt(ss * (1.0 / HEAD_DIM) + EPS) * g_ref[...])
        return jnp.concatenate(parts, axis=-1)

    def store_heads(dst_ref, rows, val):
        for h in range(N_HEADS):
            dst_ref[rows, h, :] = val[:, h * V_DIM:(h + 1) * V_DIM]

    @pl.when(j == 0)
    def _():
        h_ref[...] = _rms_rows(x_ref[...], nm_ref[...]).astype(BF16)

        def epilogue(rows, y):
            q_ref[rows, :] = (group_norm(y, qg_ref) * Q_SCALE).astype(BF16)
        section(epilogue)

    @pl.when(j == 1)
    def _():
        def epilogue(rows, y):
            kn = group_norm(y, kg_ref)
            store_heads(kf_ref, rows, kn)
            kb_ref[rows, :] = kn.astype(BF16)
        section(epilogue)

    @pl.when(j == 2)
    def _():
        def epilogue(rows, y):
            store_heads(vf_ref, rows, y)
            vb_ref[:, rows] = y.T.astype(BF16)
        section(epilogue)

    @pl.when(j == 3)
    def _():
        def epilogue(rows, y):
            u_ref[rows, :] = y
        section(epilogue)

    @pl.when(j == 4)
    def _():
        def epilogue(rows, y):
            sa_ref[rows, :] = _sigmoid(y)
        section(epilogue)

    @pl.when(j == 5)
    def _():
        def epilogue(rows, y):
            sp_ref[rows, :] = _sigmoid(y)
        section(epilogue)


def _in_projection(x2d, norm_mix, w_in_bf, q_gain, k_gain, tm):
    n, d = x2d.shape
    n_sections = w_in_bf.shape[1] // IN_SECTION
    row = lambda i, j: (i, 0)
    fixed = lambda i, j: (0, 0)
    tile = pl.BlockSpec((tm, IN_SECTION), row)
    heads_tile = pl.BlockSpec((tm, N_HEADS, V_DIM), lambda i, j: (i, 0, 0))
    flat = lambda dt: jax.ShapeDtypeStruct((n, IN_SECTION), dt)
    heads = jax.ShapeDtypeStruct((n, N_HEADS, V_DIM), F32)
    v_t_tile = pl.BlockSpec((IN_SECTION, tm), lambda i, j: (0, i))
    v_t = jax.ShapeDtypeStruct((IN_SECTION, n), BF16)
    out_specs = [tile, heads_tile, tile, heads_tile, v_t_tile, tile, tile, tile]
    out_shape = [flat(BF16), heads, flat(BF16), heads, v_t, flat(F32), flat(F32), flat(F32)]
    return pl.pallas_call(
        functools.partial(_inproj_kernel, n_split=2 if tm >= 512 else 1),
        grid=(n // tm, n_sections),
        in_specs=[pl.BlockSpec((tm, d), row),
                  pl.BlockSpec((1, d), fixed),
                  pl.BlockSpec((d, IN_SECTION), lambda i, j: (0, j)),
                  pl.BlockSpec((1, V_DIM), fixed),
                  pl.BlockSpec((1, V_DIM), fixed)],
        out_specs=out_specs,
        out_shape=out_shape,
        scratch_shapes=[pltpu.VMEM((tm, d), BF16)],
        compiler_params=_params("arbitrary", "arbitrary"),
        name="in_projection",
    )(x2d, norm_mix, w_in_bf, q_gain, k_gain)


def _prompt_attn_kernel(lam_ref, q_ref, k_ref, vt_ref, hn_ref, o_ref, m_ref, acc_ref, st0_ref, st1_ref,
                        *, blk_q, blk_k, lam_init):
    i = pl.program_id(2)
    q = q_ref[0]
    lane = lax.broadcasted_iota(jnp.int32, q.shape, 1)
    zero = jnp.zeros_like(q)
    q_comp = (jnp.where(lane < HEAD_DIM, q, zero), jnp.where(lane >= HEAD_DIM, q, zero))

    m_ref[...] = jnp.full(m_ref.shape, NEG, F32)
    acc_ref[...] = jnp.zeros(acc_ref.shape, F32)
    ones_rows = jnp.ones((SUM_ROWS, blk_k), BF16)

    def scores(j, st_ref):
        start = pl.multiple_of(j * blk_k, blk_k)
        k = k_ref[0, pl.ds(start, blk_k), :]
        for c in range(2):
            st_ref[c] = lax.dot_general(k, q_comp[c], (((1,), (1,)), ((), ())), preferred_element_type=F32)

    def update(j, st_ref, masked):
        start = pl.multiple_of(j * blk_k, blk_k)
        vt = jnp.concatenate([vt_ref[:, pl.ds(start, blk_k)], ones_rows], axis=0)
        for c in range(2):
            st = st_ref[c]
            if masked:
                k_pos = start + lax.broadcasted_iota(jnp.int32, st.shape, 0)
                q_pos = i * blk_q + lax.broadcasted_iota(jnp.int32, st.shape, 1)
                st = jnp.where(k_pos <= q_pos, st, NEG)
            m_old = m_ref[c]
            m_new = jnp.maximum(m_old, jnp.max(st, axis=0, keepdims=True))
            pt = jnp.exp2(st - m_new)
            alpha = jnp.exp2(m_old - m_new)
            acc_ref[c] = alpha * acc_ref[c] + jnp.dot(vt, pt.astype(BF16), preferred_element_type=F32)
            m_ref[c] = m_new

    def body(jj, carry):
        scores(2 * jj + 1, st1_ref)
        update(2 * jj, st0_ref, False)
        scores(2 * jj + 2, st0_ref)
        update(2 * jj + 1, st1_ref, False)
        return carry

    assert blk_q == 2 * blk_k
    scores(0, st0_ref)
    lax.fori_loop(0, i, body, 0)
    scores(2 * i + 1, st1_ref)
    update(2 * i, st0_ref, True)
    update(2 * i + 1, st1_ref, True)

    lam = _lambda_value(lam_ref, lam_init)
    norm = lambda a: a[:V_DIM] / a[V_DIM:V_DIM + 1]
    ot = norm(acc_ref[0]) - lam * norm(acc_ref[1])
    ot = ot * lax.rsqrt(jnp.mean(ot * ot, axis=0, keepdims=True) + EPS) * hn_ref[0] * (1.0 - lam_init)
    o_ref[0] = ot.T.astype(BF16)


def _prompt_attention(lam_vecs, q, k, v_t, head_norm, lam_init, blk_q, blk_k):
    b, t, _ = q.shape
    q_spec = pl.BlockSpec((1, blk_q, V_DIM), lambda bi, h, i: (bi, i, h))
    return pl.pallas_call(
        functools.partial(_prompt_attn_kernel, blk_q=blk_q, blk_k=blk_k, lam_init=lam_init),
        grid=(b, N_HEADS, t // blk_q),
        in_specs=[pl.BlockSpec(lam_vecs.shape, lambda bi, h, i: (0, 0)),
                  q_spec,
                  pl.BlockSpec((1, t, V_DIM), lambda bi, h, i: (bi, 0, h)),
                  pl.BlockSpec((V_DIM, t), lambda bi, h, i: (h, bi)),
                  pl.BlockSpec((1, V_DIM, 1), lambda bi, h, i: (h, 0, 0))],
        out_specs=q_spec,
        out_shape=jax.ShapeDtypeStruct(q.shape, BF16),
        scratch_shapes=[pltpu.VMEM((2, 1, blk_q), F32),
                        pltpu.VMEM((2, V_DIM + SUM_ROWS, blk_q), F32),
                        pltpu.VMEM((2, blk_k, blk_q), F32), pltpu.VMEM((2, blk_k, blk_q), F32)],
        compiler_params=_params("arbitrary", "arbitrary", "arbitrary"),
        name="prompt_attention",
    )(lam_vecs, q, k, v_t, head_norm.reshape(N_HEADS, V_DIM, 1))


def _decode_attn_kernel(pt_ref, lam_ref, q_ref, kn_ref, vn_ref, *refs, n_new, n_pg, lam_init):
    del pt_ref
    ck_refs, cv_refs = refs[:n_pg], refs[n_pg:2 * n_pg]
    hn_ref, o_ref, a_ref, m_ref, l_ref, acc_ref = refs[2 * n_pg:]
    p_idx = pl.program_id(1)
    n_cols = 2 * N_HEADS * n_new
    cols_per_head = 2 * n_new

    def head_of(shape, row_axis, col_axis):
        r = lax.broadcasted_iota(jnp.int32, shape, row_axis)
        c = lax.broadcasted_iota(jnp.int32, shape, col_axis)
        return r, c, c // cols_per_head

    def update(blocks, causal):
        h_idx, c_idx, c_head = head_of((N_HEADS, n_cols), 0, 1)
        own_head = h_idx == c_head
        raw = [lax.dot_general(k2d, a_ref[...], (((1,), (1,)), ((), ())), preferred_element_type=F32)
               for k2d, _ in blocks]
        scores, valids = [], []
        for s in raw:
            n_tok = s.shape[0] // N_HEADS
            s3 = s.reshape(n_tok, N_HEADS, n_cols)
            valid = own_head[None]
            if causal:
                t_idx = lax.broadcasted_iota(jnp.int32, s3.shape, 0)
                q_idx = lax.broadcasted_iota(jnp.int32, s3.shape, 2) % n_new
                valid = valid & (t_idx <= q_idx)
            scores.append(s3)
            valids.append(valid)
        m_old = m_ref[...]
        m_new = m_old
        for s3, valid in zip(scores, valids):
            m_new = jnp.maximum(m_new, jnp.max(jnp.where(valid, s3, NEG), axis=0))
        alpha = jnp.exp2(m_old - m_new)
        l_new = alpha * l_ref[...]
        pv = None
        for (_, v2d), s3, valid in zip(blocks, scores, valids):
            p3 = jnp.where(valid, jnp.exp2(s3 - m_new[None]), 0.0)
            l_new = l_new + jnp.sum(p3, axis=0)
            part = lax.dot_general(v2d, p3.reshape(v2d.shape[0], n_cols), (((0,), (0,)), ((), ())),
                                   preferred_element_type=F32)
            pv = part if pv is None else pv + part
        l_ref[...] = l_new
        m_ref[...] = m_new
        alpha_row = jnp.sum(jnp.where(own_head, alpha, 0.0), axis=0, keepdims=True)
        acc_ref[...] = acc_ref[...] * alpha_row + pv

    @pl.when(p_idx == 0)
    def _():
        q = q_ref[0].astype(F32)
        lane = lax.broadcasted_iota(jnp.int32, (n_new, V_DIM), 1)
        rows = []
        for h in range(N_HEADS):
            qh = q[:, h * V_DIM:(h + 1) * V_DIM]
            rows += [jnp.where(lane < HEAD_DIM, qh, 0.0), jnp.where(lane >= HEAD_DIM, qh, 0.0)]
        a_ref[...] = jnp.concatenate(rows, axis=0)
        m_ref[...] = jnp.full(m_ref.shape, NEG, F32)
        l_ref[...] = jnp.zeros(l_ref.shape, F32)
        acc_ref[...] = jnp.zeros(acc_ref.shape, F32)
        update([(kn_ref[0], vn_ref[0])], True)

    rows = PAGE_SIZE * N_HEADS
    pages = [(ck[0].reshape(rows, V_DIM), cv[0].reshape(rows, V_DIM)) for ck, cv in zip(ck_refs, cv_refs)]
    update(pages, False)

    @pl.when(p_idx == pl.num_programs(1) - 1)
    def _():
        lam = _lambda_value(lam_ref, lam_init)
        h_idx, c_idx, c_head = head_of((N_HEADS, n_cols), 0, 1)
        l_row = jnp.sum(jnp.where(h_idx == c_head, l_ref[...], 0.0), axis=0, keepdims=True)
        on = (acc_ref[...] / l_row).T
        parts = []
        for h in range(N_HEADS):
            blk = on[h * cols_per_head:(h + 1) * cols_per_head]
            o = blk[:n_new] - lam * blk[n_new:]
            parts.append(_rms_rows(o, hn_ref[:, h * V_DIM:(h + 1) * V_DIM]) * (1.0 - lam_init))
        o_ref[0] = jnp.concatenate(parts, axis=-1).astype(BF16)


def _decode_attention(page_table, lam_vecs, q, k_new, v_new, cache_k, cache_v, head_norm, lam_init):
    b, n_new, width = q.shape
    n_pages = page_table.shape[1]
    n_cols = 2 * N_HEADS * n_new
    seq_spec = pl.BlockSpec((1, n_new, width), lambda bi, p, pt: (bi, 0, 0))
    new_spec = pl.BlockSpec((1, n_new * N_HEADS, V_DIM), lambda bi, p, pt: (bi, 0, 0))
    n_pg = DECODE_PAGES_PER_STEP
    page_specs = [pl.BlockSpec((1, PAGE_SIZE, N_HEADS, V_DIM),
                               lambda bi, p, pt, s=s: (pt[bi, p * n_pg + s], 0, 0, 0)) for s in range(n_pg)]
    grid_spec = pltpu.PrefetchScalarGridSpec(
        num_scalar_prefetch=1,
        grid=(b, n_pages // n_pg),
        in_specs=[pl.BlockSpec(lam_vecs.shape, lambda bi, p, pt: (0, 0)),
                  seq_spec, new_spec, new_spec, *page_specs, *page_specs,
                  pl.BlockSpec((1, width), lambda bi, p, pt: (0, 0))],
        out_specs=seq_spec,
        scratch_shapes=[pltpu.VMEM((n_cols, V_DIM), F32), pltpu.VMEM((N_HEADS, n_cols), F32),
                        pltpu.VMEM((N_HEADS, n_cols), F32), pltpu.VMEM((V_DIM, n_cols), F32)],
    )
    return pl.pallas_call(
        functools.partial(_decode_attn_kernel, n_new=n_new, n_pg=n_pg, lam_init=lam_init),
        grid_spec=grid_spec,
        out_shape=jax.ShapeDtypeStruct(q.shape, BF16),
        compiler_params=_params("arbitrary", "arbitrary"),
        name="decode_attention",
    )(page_table, lam_vecs, q, k_new, v_new, *([cache_k] * n_pg), *([cache_v] * n_pg),
      head_norm.reshape(1, width))


def _pool_branch(ext, n_seq, seq_len, pos, wp_ref, ps_ref):
    rows = POOL_HALO + seq_len

    def tokens(a):
        if n_seq == 1:
            return a[POOL_HALO:]
        return a.reshape(n_seq, rows, a.shape[-1])[:, POOL_HALO:, :].reshape(n_seq * seq_len, a.shape[-1])

    cur = ext
    outs = []
    for g, w in enumerate(POOL_WINDOWS):
        cur = cur + pltpu.roll(cur, w // 2, axis=0)
        c0 = g * POOL_GROUP
        win = tokens(cur[:, :POOL_GROUP])
        tok = tokens(ext[:, c0:c0 + POOL_GROUP])
        cnt = jnp.minimum(w, pos + 1).astype(F32)
        d = (win / cnt - tok).astype(BF16)
        outs.append(jnp.dot(d, wp_ref[g], preferred_element_type=F32) * ps_ref[:, c0:c0 + POOL_GROUP])
        cur = cur[:, POOL_GROUP:]
    return jnp.concatenate(outs, axis=-1)


def _merge_project(x, o_n, pool_o, sig_a, sig_p, wa_ref, wpb_ref, wo_ref):
    attn = jnp.dot(o_n, wa_ref[...], preferred_element_type=F32)
    pool = jnp.dot(pool_o.astype(BF16), wpb_ref[...], preferred_element_type=F32)
    merged = sig_a * attn + sig_p * pool
    return x + jnp.dot(merged.astype(BF16), wo_ref[...], preferred_element_type=F32)


def _prompt_mix_kernel(u_ref, halo_ref, on_ref, sa_ref, sp_ref, x_ref, wp_ref, ps_ref, wa_ref, wpb_ref, wo_ref,
                       x1_ref, *, tm):
    i = pl.program_id(1)
    halo = jnp.where(i > 0, halo_ref[0], 0.0)
    ext = jnp.concatenate([halo, u_ref[0]], axis=0)
    pos = i * tm + lax.broadcasted_iota(jnp.int32, (tm, 1), 0)
    pool_o = _pool_branch(ext, 1, tm, pos, wp_ref, ps_ref)
    x1_ref[0] = _merge_project(x_ref[0], on_ref[0], pool_o, sa_ref[0], sp_ref[0], wa_ref, wpb_ref, wo_ref)


def _prompt_mix(u, o_n, sig_a, sig_p, x, w_pool, pool_scale, w_attn, w_poolb, w_out, tm):
    b, t, d = x.shape
    tile = lambda bi, i: (bi, i, 0)
    fixed2 = lambda bi, i: (0, 0)
    halo_blocks = tm // POOL_HALO
    tile_spec = pl.BlockSpec((1, tm, d), tile)
    return pl.pallas_call(
        functools.partial(_prompt_mix_kernel, tm=tm),
        grid=(b, t // tm),
        in_specs=[tile_spec,
                  pl.BlockSpec((1, POOL_HALO, d), lambda bi, i: (bi, jnp.maximum(i * halo_blocks - 1, 0), 0)),
                  tile_spec, tile_spec, tile_spec, tile_spec,
                  pl.BlockSpec(w_pool.shape, lambda bi, i: (0, 0, 0)),
                  pl.BlockSpec((1, d), fixed2),
                  pl.BlockSpec(w_attn.shape, fixed2),
                  pl.BlockSpec(w_poolb.shape, fixed2),
                  pl.BlockSpec(w_out.shape, fixed2)],
        out_specs=tile_spec,
        out_shape=jax.ShapeDtypeStruct(x.shape, F32),
        compiler_params=_params("arbitrary", "arbitrary"),
        name="prompt_mix",
    )(u, u, o_n, sig_a, sig_p, x, w_pool, pool_scale, w_attn, w_poolb, w_out)


def _sample_mix_kernel(u_ref, hist_ref, on_ref, sa_ref, sp_ref, x_ref, wp_ref, ps_ref, wa_ref, wpb_ref, wo_ref,
                       x1_ref, state_ref, *, n_seq, seq_len, pos0):
    d = u_ref.shape[-1]
    rows = POOL_HALO + seq_len
    ext3 = jnp.concatenate([hist_ref[...], u_ref[...].reshape(n_seq, seq_len, d)], axis=1)
    state_ref[...] = ext3[:, rows - POOL_HALO:, :]
    pos = pos0 + lax.broadcasted_iota(jnp.int32, (n_seq * seq_len, 1), 0) % seq_len
    pool_o = _pool_branch(ext3.reshape(n_seq * rows, d), n_seq, seq_len, pos, wp_ref, ps_ref)
    x1_ref[...] = _merge_project(x_ref[...], on_ref[...], pool_o, sa_ref[...], sp_ref[...], wa_ref, wpb_ref, wo_ref)


def _sample_mix(u, hist, o_n, sig_a, sig_p, x, w_pool, pool_scale, w_attn, w_poolb, w_out, seq_len, pos0):
    n, d = x.shape
    n_seq = n // seq_len
    full = lambda a: pl.BlockSpec(a.shape, lambda i, nd=a.ndim: (0,) * nd)
    args = (u, hist, o_n, sig_a, sig_p, x, w_pool, pool_scale, w_attn, w_poolb, w_out)
    out_shape = [jax.ShapeDtypeStruct(x.shape, F32), jax.ShapeDtypeStruct(hist.shape, F32)]
    return pl.pallas_call(
        functools.partial(_sample_mix_kernel, n_seq=n_seq, seq_len=seq_len, pos0=pos0),
        grid=(1,),
        in_specs=[full(a) for a in args],
        out_specs=[full(s) for s in out_shape],
        out_shape=out_shape,
        compiler_params=_params("arbitrary"),
        name="sample_mix",
    )(*args)


def _conv_taps(up, hist, n_seq, seq_len):
    c = up.shape[-1]
    rows = CONV_HALO + seq_len
    if n_seq == 1:
        ext = jnp.concatenate([hist, up], axis=0)
    else:
        ext = jnp.concatenate([hist.reshape(n_seq, CONV_HALO, c), up.reshape(n_seq, seq_len, c)], axis=1)
        ext = ext.reshape(n_seq * rows, c)

    def tokens(a):
        if n_seq == 1:
            return a[CONV_HALO:]
        return a.reshape(n_seq, rows, c)[:, CONV_HALO:, :].reshape(n_seq * seq_len, c)

    return tokens(pltpu.roll(ext, 1, axis=0)), tokens(pltpu.roll(ext, 2, axis=0))


def _ffn_kernel(x1_ref, nf_ref, wg_ref, wv_ref, cwg_ref, cwv_ref, cbg_ref, cbv_ref, wd_ref, hg_ref, hv_ref,
                y_ref, sg_ref, sv_ref, h_ref, acc_ref, carry_ref, *, n_seq, seq_len, carried, n_split):
    j = pl.program_id(2)
    n_j = pl.num_programs(2)
    assert carried or n_split == 1
    part = seq_len // n_split if carried else x1_ref.shape[0]

    @pl.when(j == 0)
    def _():
        h_ref[...] = _rms_rows(x1_ref[...], nf_ref[...]).astype(BF16)
        acc_ref[...] = jnp.zeros(acc_ref.shape, F32)

    tails = [None, None]
    for r in range(n_split):
        rows = slice(r * part, (r + 1) * part)
        h = h_ref[rows, :]
        halves = []
        for half, (w_ref, cw_ref, cb_ref, hist_ref, s_ref) in enumerate(
                ((wg_ref, cwg_ref, cbg_ref, hg_ref, sg_ref), (wv_ref, cwv_ref, cbv_ref, hv_ref, sv_ref))):
            up = jnp.dot(h, w_ref[...], preferred_element_type=F32)
            if carried:
                hist = jnp.where(pl.program_id(1) > 0, carry_ref[j, half], 0.0) if r == 0 else tails[half]
                tails[half] = up[part - CONV_HALO:]
                if r == n_split - 1:
                    carry_ref[j, half] = tails[half]
                    s_ref[0] = tails[half]
                prev1, prev2 = _conv_taps(up, hist, 1, part)
            else:
                s_ref[...] = up
                prev1, prev2 = _conv_taps(up, hist_ref[...], n_seq, seq_len)
            cw = cw_ref[...]
            halves.append(cb_ref[...] + prev2 * cw[0:1] + prev1 * cw[1:2] + up * cw[2:3])
        gate, val = halves
        act = (gate * _sigmoid(gate) * val).astype(BF16)
        acc_ref[rows, :] += jnp.dot(act, wd_ref[...], preferred_element_type=F32)

    @pl.when(j == n_j - 1)
    def _():
        y_ref[...] = x1_ref[...] + acc_ref[...]


def _conv_ffn(x1, norm_ffn, w_up, conv_w, conv_b, w_down, hist_gate, hist_val, n_batch, tm, seq_len, carried):
    n, d = x1.shape
    d_ff = w_down.shape[0]
    n_chunks = d_ff // FF_CHUNK
    tiles = n // (n_batch * tm)
    n_seq = 1 if carried else tm // seq_len
    tile_len = tm if carried else seq_len
    row = lambda bi, i, j: (bi * tiles + i, 0)
    gate_col = lambda bi, i, j: (0, j)
    val_col = lambda bi, i, j: (0, n_chunks + j)
    if carried:
        state_shape = jax.ShapeDtypeStruct((n_batch * tiles, CONV_HALO, d_ff), F32)
        state_spec = pl.BlockSpec((1, CONV_HALO, FF_CHUNK), lambda bi, i, j: (bi * tiles + i, 0, j))
        hist_spec_g = pl.BlockSpec((CONV_HALO, FF_CHUNK), lambda bi, i, j: (0, 0))
        hist_spec_v = hist_spec_g
    else:
        state_shape = jax.ShapeDtypeStruct((n, d_ff), F32)
        state_spec = pl.BlockSpec((tm, FF_CHUNK), lambda bi, i, j: (bi * tiles + i, j))
        hist_spec_g = pl.BlockSpec((n_seq * CONV_HALO, FF_CHUNK), lambda bi, i, j: (bi * tiles + i, j))
        hist_spec_v = hist_spec_g
    return pl.pallas_call(
        functools.partial(_ffn_kernel, n_seq=n_seq, seq_len=tile_len, carried=carried,
                          n_split=2 if carried else 1),
        grid=(n_batch, tiles, n_chunks),
        in_specs=[pl.BlockSpec((tm, d), row),
                  pl.BlockSpec((1, d), lambda bi, i, j: (0, 0)),
                  pl.BlockSpec((d, FF_CHUNK), gate_col),
                  pl.BlockSpec((d, FF_CHUNK), val_col),
                  pl.BlockSpec((CONV_W, FF_CHUNK), gate_col),
                  pl.BlockSpec((CONV_W, FF_CHUNK), val_col),
                  pl.BlockSpec((1, FF_CHUNK), gate_col),
                  pl.BlockSpec((1, FF_CHUNK), val_col),
                  pl.BlockSpec((FF_CHUNK, d), lambda bi, i, j: (j, 0)),
                  hist_spec_g, hist_spec_v],
        out_specs=[pl.BlockSpec((tm, d), row), state_spec, state_spec],
        out_shape=[jax.ShapeDtypeStruct((n, d), F32), state_shape, state_shape],
        scratch_shapes=[pltpu.VMEM((tm, d), BF16), pltpu.VMEM((tm, d), F32),
                        pltpu.VMEM((n_chunks, 2, CONV_HALO, FF_CHUNK), F32)],
        compiler_params=_params("arbitrary", "arbitrary", "arbitrary"),
        name="conv_ffn",
    )(x1, norm_ffn, w_up, w_up, conv_w, conv_w, conv_b, conv_b, w_down, hist_gate, hist_val)


PROMPT_TM = 512
FFN_TM = 1024
DECODE_PAGES_PER_STEP = 8
ATTN_BLK_Q = 1024
ATTN_BLK_K = 512


def kernel(x_prompt, x_sample, cache_k, cache_v, state_pool, state_ffn_conv, page_table, norm_mix, w_in, q_norm, k_norm, lambda_q1, lambda_k1, lambda_q2, lambda_k2, head_norm, w_pool, pool_scale, w_attn_branch, w_pool_branch, w_out, norm_ffn, w_up, conv_w, conv_b, w_down):
    depth = w_in.shape[0]
    b, t, d = x_prompt.shape
    bs, ts, _ = x_sample.shape
    width = N_HEADS * V_DIM
    d_ff = w_down.shape[1]
    past_len = page_table.shape[1] * PAGE_SIZE

    yp, ys = x_prompt, x_sample.reshape(bs * ts, d)
    outs = [[] for _ in range(8)]
    for l in range(depth):
        lam_init = lambda_init_for(l)
        w_in_bf = w_in[l].astype(BF16)
        w_pool_bf = w_pool[l].astype(BF16)
        w_attn_bf = w_attn_branch[l].astype(BF16)
        w_poolb_bf = w_pool_branch[l].astype(BF16)
        w_out_bf = w_out[l].astype(BF16)
        w_up_bf = w_up[l].astype(BF16)
        w_down_bf = w_down[l].astype(BF16)
        nm = norm_mix[l].reshape(1, d)
        nf = norm_ffn[l].reshape(1, d)
        qg = q_norm[l].reshape(1, V_DIM)
        kg = k_norm[l].reshape(1, V_DIM)
        lam_vecs = jnp.stack([lambda_q1[l], lambda_k1[l], lambda_q2[l], lambda_k2[l]])
        ps = pool_scale[l].reshape(1, -1)
        cb = conv_b[l].reshape(1, -1)

        q, kf, kb, vf, vb, u, sa, sp = _in_projection(yp.reshape(b * t, d), nm, w_in_bf, qg, kg, PROMPT_TM)
        r3 = lambda a: a.reshape(b, t, -1)
        o_n = _prompt_attention(lam_vecs, r3(q), r3(kb), vb, head_norm[l], lam_init, ATTN_BLK_Q, ATTN_BLK_K)
        x1 = _prompt_mix(r3(u), o_n, r3(sa), r3(sp), yp, w_pool_bf, ps, w_attn_bf, w_poolb_bf, w_out_bf, PROMPT_TM)
        no_hist = jnp.zeros((CONV_HALO, FF_CHUNK), F32)
        y2d, fg, fv = _conv_ffn(x1.reshape(b * t, d), nf, w_up_bf, conv_w[l], cb, w_down_bf, no_hist, no_hist,
                                n_batch=b, tm=FFN_TM, seq_len=t, carried=True)
        yp = y2d.reshape(b, t, d)
        outs[0].append(kf.reshape(b, t, N_HEADS, V_DIM))
        outs[1].append(vf.reshape(b, t, N_HEADS, V_DIM))
        outs[2].append(r3(u)[:, t - POOL_BUF:, :])
        last_tile = lambda a: a.reshape(b, -1, CONV_HALO, d_ff)[:, -1, CONV_HALO - (CONV_W - 1):, :]
        outs[3].append(jnp.concatenate([last_tile(fg), last_tile(fv)], axis=-1))

        n_s = bs * ts
        q, kf, kb, vf, vb, u, sa, sp = _in_projection(ys, nm, w_in_bf, qg, kg, n_s)
        s3 = lambda a: a.reshape(bs, ts, -1)
        o_n = _decode_attention(page_table, lam_vecs, s3(q),
                                kf.reshape(bs, ts * N_HEADS, V_DIM), vf.reshape(bs, ts * N_HEADS, V_DIM),
                                cache_k[l], cache_v[l], head_norm[l], lam_init)
        pool_hist = jnp.pad(state_pool[l], ((0, 0), (POOL_HALO - POOL_BUF, 0), (0, 0)))
        x1, pool_state = _sample_mix(u, pool_hist, o_n.reshape(n_s, width), sa, sp, ys, w_pool_bf, ps,
                                     w_attn_bf, w_poolb_bf, w_out_bf, ts, past_len)
        conv_hist = jnp.pad(state_ffn_conv[l], ((0, 0), (CONV_HALO - (CONV_W - 1), 0), (0, 0)))
        conv_hist = conv_hist.reshape(bs * CONV_HALO, 2 * d_ff)
        ys, fg, fv = _conv_ffn(x1, nf, w_up_bf, conv_w[l], cb, w_down_bf,
                               conv_hist[:, :d_ff], conv_hist[:, d_ff:],
                               n_batch=1, tm=n_s, seq_len=ts, carried=False)
        outs[4].append(kf.reshape(bs, ts, N_HEADS, V_DIM))
        outs[5].append(vf.reshape(bs, ts, N_HEADS, V_DIM))
        outs[6].append(pool_state[:, POOL_HALO - POOL_BUF:, :])
        ffn_state = jnp.concatenate([fg, fv], axis=-1).reshape(bs, ts, 2 * d_ff)
        outs[7].append(ffn_state[:, ts - (CONV_W - 1):, :])

    return (yp, ys.reshape(bs, ts, d)) + tuple(jnp.stack(o) for o in outs)
```

```python
import functools
import math

import jax
import jax.numpy as jnp
from jax import lax
from jax.experimental import pallas as pl
from jax.experimental.pallas import tpu as pltpu

F32 = jnp.float32
BF16 = jnp.bfloat16

N_HEADS = 8
HEAD_DIM = 64
V_DIM = 2 * HEAD_DIM
POOL_WINDOWS = (2, 4, 8, 16)
POOL_GROUP = 256
POOL_BUF = max(POOL_WINDOWS) - 1
POOL_HALO = 16
CONV_W = 3
CONV_HALO = 8
PAGE_SIZE = 128
SUM_ROWS = 16
EPS = 1e-6
NEG = -1e30
Q_SCALE = HEAD_DIM ** -0.5 * math.log2(math.e)
FF_CHUNK = 256
FFN_ROW_PARTS = 2
IN_SECTION = 1024
IN_PART_ROWS = 256
VMEM_LIMIT = 56 * 1024 * 1024


def lambda_init_for(layer):
    return 0.8 - 0.6 * math.exp(-0.3 * layer)


def _params(*sem):
    return pltpu.CompilerParams(dimension_semantics=sem, vmem_limit_bytes=VMEM_LIMIT)


def _sigmoid(x):
    return 1.0 / (1.0 + jnp.exp(-x))


def _rms_rows(x, g):
    return x * lax.rsqrt(jnp.mean(x * x, axis=-1, keepdims=True) + EPS) * g


def _lambda_value(lam_ref, lam_init):
    lv = lam_ref[...]
    s1 = jnp.sum(lv[0:1] * lv[1:2], axis=-1, keepdims=True)
    s2 = jnp.sum(lv[2:3] * lv[3:4], axis=-1, keepdims=True)
    return jnp.exp(s1) - jnp.exp(s2) + lam_init


def _inproj_kernel(x_ref, nm_ref, w_ref, qg_ref, kg_ref,
                   q_ref, kf_ref, kb_ref, vf_ref, vb_ref, u_ref, sa_ref, sp_ref, h_ref, *, n_split):
    j = pl.program_id(1)
    part = x_ref.shape[0] // n_split

    def section(epilogue):
        for r in range(n_split):
            rows = slice(r * part, (r + 1) * part)
            epilogue(rows, jnp.dot(h_ref[rows, :], w_ref[...], preferred_element_type=F32))

    def group_norm(y, g_ref):
        r = lax.broadcasted_iota(jnp.int32, (V_DIM, V_DIM), 0) // HEAD_DIM
        c = lax.broadcasted_iota(jnp.int32, (V_DIM, V_DIM), 1) // HEAD_DIM
        ones_bd = (r == c).astype(BF16)
        parts = []
        for h in range(N_HEADS):
            yh = y[:, h * V_DIM:(h + 1) * V_DIM]
            ss = jnp.dot((yh * yh).astype(BF16), ones_bd, preferred_element_type=F32)
            parts.append(yh * lax.rsqrt(ss * (1.0 / HEAD_DIM) + EPS) * g_ref[...])
        return jnp.concatenate(parts, axis=-1)

    def store_heads(dst_ref, rows, val):
        for h in range(N_HEADS):
            dst_ref[rows, h, :] = val[:, h * V_DIM:(h + 1) * V_DIM]

    @pl.when(j == 0)
    def _():
        h_ref[...] = _rms_rows(x_ref[...], nm_ref[...]).astype(BF16)

        def epilogue(rows, y):
            q_ref[rows, :] = (group_norm(y, qg_ref) * Q_SCALE).astype(BF16)
        section(epilogue)

    @pl.when(j == 1)
    def _():
        def epilogue(rows, y):
            kn = group_norm(y, kg_ref)
            store_heads(kf_ref, rows, kn)
            kb_ref[rows, :] = kn.astype(BF16)
        section(epilogue)

    @pl.when(j == 2)
    def _():
        def epilogue(rows, y):
            store_heads(vf_ref, rows, y)
            vb_ref[:, rows] = y.T.astype(BF16)
        section(epilogue)

    @pl.when(j == 3)
    def _():
        def epilogue(rows, y):
            u_ref[rows, :] = y
        section(epilogue)

    @pl.when(j == 4)
    def _():
        def epilogue(rows, y):
            sa_ref[rows, :] = _sigmoid(y).astype(BF16)
        section(epilogue)

    @pl.when(j == 5)
    def _():
        def epilogue(rows, y):
            sp_ref[rows, :] = _sigmoid(y).astype(BF16)
        section(epilogue)


def _in_projection(x2d, norm_mix, w_in_bf, q_gain, k_gain, tm):
    n, d = x2d.shape
    n_sections = w_in_bf.shape[1] // IN_SECTION
    row = lambda i, j: (i, 0)
    fixed = lambda i, j: (0, 0)
    tile = pl.BlockSpec((tm, IN_SECTION), row)
    heads_tile = pl.BlockSpec((tm, N_HEADS, V_DIM), lambda i, j: (i, 0, 0))
    flat = lambda dt: jax.ShapeDtypeStruct((n, IN_SECTION), dt)
    heads = jax.ShapeDtypeStruct((n, N_HEADS, V_DIM), F32)
    v_t_tile = pl.BlockSpec((IN_SECTION, tm), lambda i, j: (0, i))
    v_t = jax.ShapeDtypeStruct((IN_SECTION, n), BF16)
    out_specs = [tile, heads_tile, tile, heads_tile, v_t_tile, tile, tile, tile]
    out_shape = [flat(BF16), heads, flat(BF16), heads, v_t, flat(F32), flat(BF16), flat(BF16)]
    return pl.pallas_call(
        functools.partial(_inproj_kernel, n_split=max(1, tm // IN_PART_ROWS)),
        grid=(n // tm, n_sections),
        in_specs=[pl.BlockSpec((tm, d), row),
                  pl.BlockSpec((1, d), fixed),
                  pl.BlockSpec((d, IN_SECTION), lambda i, j: (0, j)),
                  pl.BlockSpec((1, V_DIM), fixed),
                  pl.BlockSpec((1, V_DIM), fixed)],
        out_specs=out_specs,
        out_shape=out_shape,
        scratch_shapes=[pltpu.VMEM((tm, d), BF16)],
        compiler_params=_params("arbitrary", "arbitrary"),
        name="in_projection",
    )(x2d, norm_mix, w_in_bf, q_gain, k_gain)


def _prompt_attn_kernel(lam_ref, q_ref, k_ref, vt_ref, hn_ref, o_ref, m_ref, acc_ref, st0_ref, st1_ref,
                        *, blk_q, blk_k, lam_init):
    i = pl.program_id(2)
    q = q_ref[0]
    lane = lax.broadcasted_iota(jnp.int32, q.shape, 1)
    zero = jnp.zeros_like(q)
    q_comp = (jnp.where(lane < HEAD_DIM, q, zero), jnp.where(lane >= HEAD_DIM, q, zero))

    m_ref[...] = jnp.full(m_ref.shape, NEG, F32)
    acc_ref[...] = jnp.zeros(acc_ref.shape, F32)
    ones_rows = jnp.ones((SUM_ROWS, blk_k), BF16)

    def scores(j, st_ref, q_lo=0):
        start = pl.multiple_of(j * blk_k, blk_k)
        k = k_ref[0, pl.ds(start, blk_k), :]
        for c in range(2):
            st_ref[c, :, q_lo:] = lax.dot_general(k, q_comp[c][q_lo:], (((1,), (1,)), ((), ())),
                                                  preferred_element_type=F32)

    def update(j, st_ref, masked, q_lo=0):
        start = pl.multiple_of(j * blk_k, blk_k)
        vt = jnp.concatenate([vt_ref[:, pl.ds(start, blk_k)], ones_rows], axis=0)
        for c in range(2):
            st = st_ref[c, :, q_lo:]
            if masked:
                k_pos = start + lax.broadcasted_iota(jnp.int32, st.shape, 0)
                q_pos = i * blk_q + q_lo + lax.broadcasted_iota(jnp.int32, st.shape, 1)
                st = jnp.where(k_pos <= q_pos, st, NEG)
            m_old = m_ref[c, :, q_lo:]
            m_new = jnp.maximum(m_old, jnp.max(st, axis=0, keepdims=True))
            pt = jnp.exp2(st - m_new)
            alpha = jnp.exp2(m_old - m_new)
            acc_ref[c, :, q_lo:] = (alpha * acc_ref[c, :, q_lo:]
                                    + jnp.dot(vt, pt.astype(BF16), preferred_element_type=F32))
            m_ref[c, :, q_lo:] = m_new

    def body(jj, carry):
        scores(2 * jj + 1, st1_ref)
        update(2 * jj, st0_ref, False)
        scores(2 * jj + 2, st0_ref)
        update(2 * jj + 1, st1_ref, False)
        return carry

    assert blk_q == 2 * blk_k
    scores(0, st0_ref)
    lax.fori_loop(0, i, body, 0)
    scores(2 * i + 1, st1_ref, q_lo=blk_k)
    update(2 * i, st0_ref, True)
    update(2 * i + 1, st1_ref, True, q_lo=blk_k)

    lam = _lambda_value(lam_ref, lam_init)
    norm = lambda a: a[:V_DIM] / a[V_DIM:V_DIM + 1]
    ot = norm(acc_ref[0]) - lam * norm(acc_ref[1])
    ot = ot * lax.rsqrt(jnp.mean(ot * ot, axis=0, keepdims=True) + EPS) * hn_ref[0] * (1.0 - lam_init)
    o_ref[0] = ot.T.astype(BF16)


def _prompt_attention(lam_vecs, q, k, v_t, head_norm, lam_init, blk_q, blk_k):
    b, t, _ = q.shape
    q_spec = pl.BlockSpec((1, blk_q, V_DIM), lambda bi, h, i: (bi, i, h))
    return pl.pallas_call(
        functools.partial(_prompt_attn_kernel, blk_q=blk_q, blk_k=blk_k, lam_init=lam_init),
        grid=(b, N_HEADS, t // blk_q),
        in_specs=[pl.BlockSpec(lam_vecs.shape, lambda bi, h, i: (0, 0)),
                  q_spec,
                  pl.BlockSpec((1, t, V_DIM), lambda bi, h, i: (bi, 0, h)),
                  pl.BlockSpec((V_DIM, t), lambda bi, h, i: (h, bi)),
                  pl.BlockSpec((1, V_DIM, 1), lambda bi, h, i: (h, 0, 0))],
        out_specs=q_spec,
        out_shape=jax.ShapeDtypeStruct(q.shape, BF16),
        scratch_shapes=[pltpu.VMEM((2, 1, blk_q), F32),
                        pltpu.VMEM((2, V_DIM + SUM_ROWS, blk_q), F32),
                        pltpu.VMEM((2, blk_k, blk_q), F32), pltpu.VMEM((2, blk_k, blk_q), F32)],
        compiler_params=_params("arbitrary", "arbitrary", "arbitrary"),
        name="prompt_attention",
    )(lam_vecs, q, k, v_t, head_norm.reshape(N_HEADS, V_DIM, 1))


def _decode_attn_kernel(pt_ref, lam_ref, q_ref, kn_ref, vn_ref, *refs, n_new, n_pg, lam_init):
    del pt_ref
    ck_refs, cv_refs = refs[:n_pg], refs[n_pg:2 * n_pg]
    hn_ref, o_ref, a_ref, m_ref, l_ref, acc_ref = refs[2 * n_pg:]
    p_idx = pl.program_id(1)
    n_cols = 2 * N_HEADS * n_new
    cols_per_head = 2 * n_new

    def head_of(shape, row_axis, col_axis):
        r = lax.broadcasted_iota(jnp.int32, shape, row_axis)
        c = lax.broadcasted_iota(jnp.int32, shape, col_axis)
        return r, c, c // cols_per_head

    def update(blocks, causal):
        h_idx, c_idx, c_head = head_of((N_HEADS, n_cols), 0, 1)
        own_head = h_idx == c_head
        raw = [lax.dot_general(k2d, a_ref[...], (((1,), (1,)), ((), ())), preferred_element_type=F32)
               for k2d, _ in blocks]
        scores, valids = [], []
        for s in raw:
            n_tok = s.shape[0] // N_HEADS
            s3 = s.reshape(n_tok, N_HEADS, n_cols)
            valid = own_head[None]
            if causal:
                t_idx = lax.broadcasted_iota(jnp.int32, s3.shape, 0)
                q_idx = lax.broadcasted_iota(jnp.int32, s3.shape, 2) % n_new
                valid = valid & (t_idx <= q_idx)
            scores.append(s3)
            valids.append(valid)
        m_old = m_ref[...]
        m_new = m_old
        for s3, valid in zip(scores, valids):
            m_new = jnp.maximum(m_new, jnp.max(jnp.where(valid, s3, NEG), axis=0))
        alpha = jnp.exp2(m_old - m_new)
        l_new = alpha * l_ref[...]
        pv = None
        for (_, v2d), s3, valid in zip(blocks, scores, valids):
            p3 = jnp.where(valid, jnp.exp2(s3 - m_new[None]), 0.0)
            l_new = l_new + jnp.sum(p3, axis=0)
            part = lax.dot_general(v2d, p3.reshape(v2d.shape[0], n_cols), (((0,), (0,)), ((), ())),
                                   preferred_element_type=F32)
            pv = part if pv is None else pv + part
        l_ref[...] = l_new
        m_ref[...] = m_new
        alpha_row = jnp.sum(jnp.where(own_head, alpha, 0.0), axis=0, keepdims=True)
        acc_ref[...] = acc_ref[...] * alpha_row + pv

    @pl.when(p_idx == 0)
    def _():
        q = q_ref[0].astype(F32)
        lane = lax.broadcasted_iota(jnp.int32, (n_new, V_DIM), 1)
        rows = []
        for h in range(N_HEADS):
            qh = q[:, h * V_DIM:(h + 1) * V_DIM]
            rows += [jnp.where(lane < HEAD_DIM, qh, 0.0), jnp.where(lane >= HEAD_DIM, qh, 0.0)]
        a_ref[...] = jnp.concatenate(rows, axis=0)
        m_ref[...] = jnp.full(m_ref.shape, NEG, F32)
        l_ref[...] = jnp.zeros(l_ref.shape, F32)
        acc_ref[...] = jnp.zeros(acc_ref.shape, F32)
        update([(kn_ref[0], vn_ref[0])], True)

    rows = PAGE_SIZE * N_HEADS
    pages = [(ck[0].reshape(rows, V_DIM), cv[0].reshape(rows, V_DIM)) for ck, cv in zip(ck_refs, cv_refs)]
    update(pages, False)

    @pl.when(p_idx == pl.num_programs(1) - 1)
    def _():
        lam = _lambda_value(lam_ref, lam_init)
        h_idx, c_idx, c_head = head_of((N_HEADS, n_cols), 0, 1)
        l_row = jnp.sum(jnp.where(h_idx == c_head, l_ref[...], 0.0), axis=0, keepdims=True)
        on = (acc_ref[...] / l_row).T
        parts = []
        for h in range(N_HEADS):
            blk = on[h * cols_per_head:(h + 1) * cols_per_head]
            o = blk[:n_new] - lam * blk[n_new:]
            parts.append(_rms_rows(o, hn_ref[:, h * V_DIM:(h + 1) * V_DIM]) * (1.0 - lam_init))
        o_ref[0] = jnp.concatenate(parts, axis=-1).astype(BF16)


def _decode_attention(page_table, lam_vecs, q, k_new, v_new, cache_k, cache_v, head_norm, lam_init):
    b, n_new, width = q.shape
    n_pages = page_table.shape[1]
    n_cols = 2 * N_HEADS * n_new
    seq_spec = pl.BlockSpec((1, n_new, width), lambda bi, p, pt: (bi, 0, 0))
    new_spec = pl.BlockSpec((1, n_new * N_HEADS, V_DIM), lambda bi, p, pt: (bi, 0, 0))
    n_pg = DECODE_PAGES_PER_STEP
    page_specs = [pl.BlockSpec((1, PAGE_SIZE, N_HEADS, V_DIM),
                               lambda bi, p, pt, s=s: (pt[bi, p * n_pg + s], 0, 0, 0)) for s in range(n_pg)]
    grid_spec = pltpu.PrefetchScalarGridSpec(
        num_scalar_prefetch=1,
        grid=(b, n_pages // n_pg),
        in_specs=[pl.BlockSpec(lam_vecs.shape, lambda bi, p, pt: (0, 0)),
                  seq_spec, new_spec, new_spec, *page_specs, *page_specs,
                  pl.BlockSpec((1, width), lambda bi, p, pt: (0, 0))],
        out_specs=seq_spec,
        scratch_shapes=[pltpu.VMEM((n_cols, V_DIM), F32), pltpu.VMEM((N_HEADS, n_cols), F32),
                        pltpu.VMEM((N_HEADS, n_cols), F32), pltpu.VMEM((V_DIM, n_cols), F32)],
    )
    return pl.pallas_call(
        functools.partial(_decode_attn_kernel, n_new=n_new, n_pg=n_pg, lam_init=lam_init),
        grid_spec=grid_spec,
        out_shape=jax.ShapeDtypeStruct(q.shape, BF16),
        compiler_params=_params("arbitrary", "arbitrary"),
        name="decode_attention",
    )(page_table, lam_vecs, q, k_new, v_new, *([cache_k] * n_pg), *([cache_v] * n_pg),
      head_norm.reshape(1, width))


def _pool_branch(ext, n_seq, seq_len, pos, wp_ref, ps_ref):
    rows = POOL_HALO + seq_len

    def tokens(a):
        if n_seq == 1:
            return a[POOL_HALO:]
        return a.reshape(n_seq, rows, a.shape[-1])[:, POOL_HALO:, :].reshape(n_seq * seq_len, a.shape[-1])

    cur = ext
    outs = []
    for g, w in enumerate(POOL_WINDOWS):
        cur = cur + pltpu.roll(cur, w // 2, axis=0)
        c0 = g * POOL_GROUP
        win = tokens(cur[:, :POOL_GROUP])
        tok = tokens(ext[:, c0:c0 + POOL_GROUP])
        cnt = jnp.minimum(w, pos + 1).astype(F32)
        d = (win / cnt - tok).astype(BF16)
        outs.append(jnp.dot(d, wp_ref[g], preferred_element_type=F32) * ps_ref[:, c0:c0 + POOL_GROUP])
        cur = cur[:, POOL_GROUP:]
    return jnp.concatenate(outs, axis=-1)


def _merge_project(x, o_n, pool_o, sig_a, sig_p, wa_ref, wpb_ref, wo_ref):
    attn = jnp.dot(o_n, wa_ref[...], preferred_element_type=F32)
    pool = jnp.dot(pool_o.astype(BF16), wpb_ref[...], preferred_element_type=F32)
    merged = sig_a * attn + sig_p * pool
    return x + jnp.dot(merged.astype(BF16), wo_ref[...], preferred_element_type=F32)


def _prompt_mix_kernel(u_ref, halo_ref, on_ref, sa_ref, sp_ref, x_ref, wp_ref, ps_ref, wa_ref, wpb_ref, wo_ref,
                       x1_ref, *, tm):
    i = pl.program_id(1)
    halo = jnp.where(i > 0, halo_ref[0], 0.0)
    ext = jnp.concatenate([halo, u_ref[0]], axis=0)
    pos = i * tm + lax.broadcasted_iota(jnp.int32, (tm, 1), 0)
    pool_o = _pool_branch(ext, 1, tm, pos, wp_ref, ps_ref)
    x1_ref[0] = _merge_project(x_ref[0], on_ref[0], pool_o, sa_ref[0], sp_ref[0], wa_ref, wpb_ref, wo_ref)


def _prompt_mix(u, o_n, sig_a, sig_p, x, w_pool, pool_scale, w_attn, w_poolb, w_out, tm):
    b, t, d = x.shape
    tile = lambda bi, i: (bi, i, 0)
    fixed2 = lambda bi, i: (0, 0)
    halo_blocks = tm // POOL_HALO
    tile_spec = pl.BlockSpec((1, tm, d), tile)
    return pl.pallas_call(
        functools.partial(_prompt_mix_kernel, tm=tm),
        grid=(b, t // tm),
        in_specs=[tile_spec,
                  pl.BlockSpec((1, POOL_HALO, d), lambda bi, i: (bi, jnp.maximum(i * halo_blocks - 1, 0), 0)),
                  tile_spec, tile_spec, tile_spec, tile_spec,
                  pl.BlockSpec(w_pool.shape, lambda bi, i: (0, 0, 0)),
                  pl.BlockSpec((1, d), fixed2),
                  pl.BlockSpec(w_attn.shape, fixed2),
                  pl.BlockSpec(w_poolb.shape, fixed2),
                  pl.BlockSpec(w_out.shape, fixed2)],
        out_specs=tile_spec,
        out_shape=jax.ShapeDtypeStruct(x.shape, F32),
        compiler_params=_params("arbitrary", "arbitrary"),
        name="prompt_mix",
    )(u, u, o_n, sig_a, sig_p, x, w_pool, pool_scale, w_attn, w_poolb, w_out)


def _sample_mix_kernel(u_ref, hist_ref, on_ref, sa_ref, sp_ref, x_ref, wp_ref, ps_ref, wa_ref, wpb_ref, wo_ref,
                       x1_ref, state_ref, *, n_seq, seq_len, pos0):
    d = u_ref.shape[-1]
    rows = POOL_HALO + seq_len
    ext3 = jnp.concatenate([hist_ref[...], u_ref[...].reshape(n_seq, seq_len, d)], axis=1)
    state_ref[...] = ext3[:, rows - POOL_HALO:, :]
    pos = pos0 + lax.broadcasted_iota(jnp.int32, (n_seq * seq_len, 1), 0) % seq_len
    pool_o = _pool_branch(ext3.reshape(n_seq * rows, d), n_seq, seq_len, pos, wp_ref, ps_ref)
    x1_ref[...] = _merge_project(x_ref[...], on_ref[...], pool_o, sa_ref[...], sp_ref[...], wa_ref, wpb_ref, wo_ref)


def _sample_mix(u, hist, o_n, sig_a, sig_p, x, w_pool, pool_scale, w_attn, w_poolb, w_out, seq_len, pos0):
    n, d = x.shape
    n_seq = n // seq_len
    full = lambda a: pl.BlockSpec(a.shape, lambda i, nd=a.ndim: (0,) * nd)
    args = (u, hist, o_n, sig_a, sig_p, x, w_pool, pool_scale, w_attn, w_poolb, w_out)
    out_shape = [jax.ShapeDtypeStruct(x.shape, F32), jax.ShapeDtypeStruct(hist.shape, F32)]
    return pl.pallas_call(
        functools.partial(_sample_mix_kernel, n_seq=n_seq, seq_len=seq_len, pos0=pos0),
        grid=(1,),
        in_specs=[full(a) for a in args],
        out_specs=[full(s) for s in out_shape],
        out_shape=out_shape,
        compiler_params=_params("arbitrary"),
        name="sample_mix",
    )(*args)


def _conv_taps(up, hist, n_seq, seq_len):
    c = up.shape[-1]
    rows = CONV_HALO + seq_len
    if n_seq == 1:
        ext = jnp.concatenate([hist, up], axis=0)
    else:
        ext = jnp.concatenate([hist.reshape(n_seq, CONV_HALO, c), up.reshape(n_seq, seq_len, c)], axis=1)
        ext = ext.reshape(n_seq * rows, c)

    def tokens(a):
        if n_seq == 1:
            return a[CONV_HALO:]
        return a.reshape(n_seq, rows, c)[:, CONV_HALO:, :].reshape(n_seq * seq_len, c)

    return tokens(pltpu.roll(ext, 1, axis=0)), tokens(pltpu.roll(ext, 2, axis=0))


def _ffn_kernel(x1_ref, nf_ref, wg_ref, wv_ref, cwg_ref, cwv_ref, cbg_ref, cbv_ref, wd_ref, hg_ref, hv_ref,
                y_ref, sg_ref, sv_ref, h_ref, acc_ref, *, n_seq, seq_len):
    j = pl.program_id(1)
    n_j = pl.num_programs(1)

    @pl.when(j == 0)
    def _():
        h_ref[...] = _rms_rows(x1_ref[...], nf_ref[...]).astype(BF16)
        acc_ref[...] = jnp.zeros(acc_ref.shape, F32)

    h = h_ref[...]
    halves = []
    for w_ref, cw_ref, cb_ref, hist_ref, s_ref in ((wg_ref, cwg_ref, cbg_ref, hg_ref, sg_ref),
                                                   (wv_ref, cwv_ref, cbv_ref, hv_ref, sv_ref)):
        up = jnp.dot(h, w_ref[...], preferred_element_type=F32)
        s_ref[...] = up
        prev1, prev2 = _conv_taps(up, hist_ref[...], n_seq, seq_len)
        cw = cw_ref[...]
        halves.append(cb_ref[...] + prev2 * cw[0:1] + prev1 * cw[1:2] + up * cw[2:3])
    gate, val = halves
    act = (gate * _sigmoid(gate) * val).astype(BF16)
    acc_ref[...] += jnp.dot(act, wd_ref[...], preferred_element_type=F32)

    @pl.when(j == n_j - 1)
    def _():
        y_ref[...] = x1_ref[...] + acc_ref[...]


def _conv_ffn(x1, norm_ffn, w_up, conv_w, conv_b, w_down, hist_gate, hist_val, tm, seq_len):
    n, d = x1.shape
    d_ff = w_down.shape[0]
    n_chunks = d_ff // FF_CHUNK
    n_seq = tm // seq_len
    row = lambda i, j: (i, 0)
    gate_col = lambda i, j: (0, j)
    val_col = lambda i, j: (0, n_chunks + j)
    up_shape = jax.ShapeDtypeStruct((n, d_ff), F32)
    up_spec = pl.BlockSpec((tm, FF_CHUNK), lambda i, j: (i, j))
    hist_spec = pl.BlockSpec((n_seq * CONV_HALO, FF_CHUNK), lambda i, j: (i, j))
    return pl.pallas_call(
        functools.partial(_ffn_kernel, n_seq=n_seq, seq_len=seq_len),
        grid=(n // tm, n_chunks),
        in_specs=[pl.BlockSpec((tm, d), row),
                  pl.BlockSpec((1, d), lambda i, j: (0, 0)),
                  pl.BlockSpec((d, FF_CHUNK), gate_col),
                  pl.BlockSpec((d, FF_CHUNK), val_col),
                  pl.BlockSpec((CONV_W, FF_CHUNK), gate_col),
                  pl.BlockSpec((CONV_W, FF_CHUNK), val_col),
                  pl.BlockSpec((1, FF_CHUNK), gate_col),
                  pl.BlockSpec((1, FF_CHUNK), val_col),
                  pl.BlockSpec((FF_CHUNK, d), lambda i, j: (j, 0)),
                  hist_spec, hist_spec],
        out_specs=[pl.BlockSpec((tm, d), row), up_spec, up_spec],
        out_shape=[jax.ShapeDtypeStruct((n, d), F32), up_shape, up_shape],
        scratch_shapes=[pltpu.VMEM((tm, d), BF16), pltpu.VMEM((tm, d), F32)],
        compiler_params=_params("arbitrary", "arbitrary"),
        name="conv_ffn",
    )(x1, norm_ffn, w_up, w_up, conv_w, conv_w, conv_b, conv_b, w_down, hist_gate, hist_val)


def _ffn_seq_kernel(x1_ref, nf_ref, wu_ref, cw_ref, cb_ref, wd_ref, y_ref, state_ref,
                    h_ref, acc_ref, carry_ref, up0_ref, up1_ref):
    i = pl.program_id(1)
    tm = x1_ref.shape[0]
    d_ff = wd_ref.shape[0]
    n_chunks = d_ff // FF_CHUNK
    h_ref[...] = _rms_rows(x1_ref[...], nf_ref[...]).astype(BF16)
    acc_ref[...] = jnp.zeros(acc_ref.shape, F32)
    up_refs = (up0_ref, up1_ref)

    def cols(c, half):
        return slice(half * d_ff + c * FF_CHUNK, half * d_ff + (c + 1) * FF_CHUNK)

    def up_project(c):
        for half in range(2):
            up_refs[c % 2][half] = jnp.dot(h_ref[...], wu_ref[:, cols(c, half)], preferred_element_type=F32)

    def consume(c):
        part = tm // FFN_ROW_PARTS
        hists = [jnp.where(i > 0, carry_ref[:, cols(c, half)], 0.0) for half in range(2)]
        for r in range(FFN_ROW_PARTS):
            rows = slice(r * part, (r + 1) * part)
            halves = []
            for half in range(2):
                up = up_refs[c % 2][half, rows, :]
                prev1, prev2 = _conv_taps(up, hists[half], 1, part)
                hists[half] = up[part - CONV_HALO:]
                cw = cw_ref[:, cols(c, half)]
                halves.append(cb_ref[:, cols(c, half)] + prev2 * cw[0:1] + prev1 * cw[1:2] + up * cw[2:3])
            gate, val = halves
            act = (gate * _sigmoid(gate) * val).astype(BF16)
            acc_ref[rows, :] += jnp.dot(act, wd_ref[c * FF_CHUNK:(c + 1) * FF_CHUNK, :],
                                        preferred_element_type=F32)
        for half in range(2):
            carry_ref[:, cols(c, half)] = hists[half]
            state_ref[0, :, cols(c, half)] = hists[half]

    up_project(0)
    for c in range(n_chunks):
        if c + 1 < n_chunks:
            up_project(c + 1)
        consume(c)
    y_ref[...] = x1_ref[...] + acc_ref[...]


def _conv_ffn_seq(x1, norm_ffn, w_up, conv_w, conv_b, w_down, n_batch, tm):
    n, d = x1.shape
    d_ff = w_down.shape[0]
    tiles = n // (n_batch * tm)
    row = lambda bi, i: (bi * tiles + i, 0)
    whole = lambda a: pl.BlockSpec(a.shape, lambda bi, i, nd=a.ndim: (0,) * nd, pipeline_mode=pl.Buffered(1))
    state_shape = (n_batch * tiles, CONV_HALO, 2 * d_ff)
    return pl.pallas_call(
        _ffn_seq_kernel,
        grid=(n_batch, tiles),
        in_specs=[pl.BlockSpec((tm, d), row), whole(norm_ffn), whole(w_up), whole(conv_w), whole(conv_b),
                  whole(w_down)],
        out_specs=[pl.BlockSpec((tm, d), row),
                   pl.BlockSpec((1,) + state_shape[1:], lambda bi, i: (bi * tiles + i, 0, 0))],
        out_shape=[jax.ShapeDtypeStruct((n, d), F32), jax.ShapeDtypeStruct(state_shape, F32)],
        scratch_shapes=[pltpu.VMEM((tm, d), BF16), pltpu.VMEM((tm, d), F32),
                        pltpu.VMEM((CONV_HALO, 2 * d_ff), F32),
                        pltpu.VMEM((2, tm, FF_CHUNK), F32), pltpu.VMEM((2, tm, FF_CHUNK), F32)],
        compiler_params=_params("arbitrary", "arbitrary"),
        name="conv_ffn_seq",
    )(x1, norm_ffn, w_up, conv_w, conv_b, w_down)


PROMPT_TM = 512
FFN_TM = 512
DECODE_PAGES_PER_STEP = 8
ATTN_BLK_Q = 1024
ATTN_BLK_K = 512


def kernel(x_prompt, x_sample, cache_k, cache_v, state_pool, state_ffn_conv, page_table, norm_mix, w_in, q_norm, k_norm, lambda_q1, lambda_k1, lambda_q2, lambda_k2, head_norm, w_pool, pool_scale, w_attn_branch, w_pool_branch, w_out, norm_ffn, w_up, conv_w, conv_b, w_down):
    depth = w_in.shape[0]
    b, t, d = x_prompt.shape
    bs, ts, _ = x_sample.shape
    width = N_HEADS * V_DIM
    d_ff = w_down.shape[1]
    past_len = page_table.shape[1] * PAGE_SIZE

    yp, ys = x_prompt, x_sample.reshape(bs * ts, d)
    outs = [[] for _ in range(8)]
    for l in range(depth):
        lam_init = lambda_init_for(l)
        w_in_bf = w_in[l].astype(BF16)
        w_pool_bf = w_pool[l].astype(BF16)
        w_attn_bf = w_attn_branch[l].astype(BF16)
        w_poolb_bf = w_pool_branch[l].astype(BF16)
        w_out_bf = w_out[l].astype(BF16)
        w_up_bf = w_up[l].astype(BF16)
        w_down_bf = w_down[l].astype(BF16)
        nm = norm_mix[l].reshape(1, d)
        nf = norm_ffn[l].reshape(1, d)
        qg = q_norm[l].reshape(1, V_DIM)
        kg = k_norm[l].reshape(1, V_DIM)
        lam_vecs = jnp.stack([lambda_q1[l], lambda_k1[l], lambda_q2[l], lambda_k2[l]])
        ps = pool_scale[l].reshape(1, -1)
        cb = conv_b[l].reshape(1, -1)

        q, kf, kb, vf, vb, u, sa, sp = _in_projection(yp.reshape(b * t, d), nm, w_in_bf, qg, kg, PROMPT_TM)
        r3 = lambda a: a.reshape(b, t, -1)
        o_n = _prompt_attention(lam_vecs, r3(q), r3(kb), vb, head_norm[l], lam_init, ATTN_BLK_Q, ATTN_BLK_K)
        x1 = _prompt_mix(r3(u), o_n, r3(sa), r3(sp), yp, w_pool_bf, ps, w_attn_bf, w_poolb_bf, w_out_bf, PROMPT_TM)
        y2d, f_tails = _conv_ffn_seq(x1.reshape(b * t, d), nf, w_up_bf, conv_w[l], cb, w_down_bf, b, FFN_TM)
        yp = y2d.reshape(b, t, d)
        outs[0].append(kf.reshape(b, t, N_HEADS, V_DIM))
        outs[1].append(vf.reshape(b, t, N_HEADS, V_DIM))
        outs[2].append(r3(u)[:, t - POOL_BUF:, :])
        last = f_tails.reshape((b, -1) + f_tails.shape[1:])[:, -1]
        outs[3].append(last[:, CONV_HALO - (CONV_W - 1):, :])

        n_s = bs * ts
        q, kf, kb, vf, vb, u, sa, sp = _in_projection(ys, nm, w_in_bf, qg, kg, n_s)
        s3 = lambda a: a.reshape(bs, ts, -1)
        o_n = _decode_attention(page_table, lam_vecs, s3(q),
                                kf.reshape(bs, ts * N_HEADS, V_DIM), vf.reshape(bs, ts * N_HEADS, V_DIM),
                                cache_k[l], cache_v[l], head_norm[l], lam_init)
        pool_hist = jnp.pad(state_pool[l], ((0, 0), (POOL_HALO - POOL_BUF, 0), (0, 0)))
        x1, pool_state = _sample_mix(u, pool_hist, o_n.reshape(n_s, width), sa, sp, ys, w_pool_bf, ps,
                                     w_attn_bf, w_poolb_bf, w_out_bf, ts, past_len)
        conv_hist = jnp.pad(state_ffn_conv[l], ((0, 0), (CONV_HALO - (CONV_W - 1), 0), (0, 0)))
        conv_hist = conv_hist.reshape(bs * CONV_HALO, 2 * d_ff)
        ys, fg, fv = _conv_ffn(x1, nf, w_up_bf, conv_w[l], cb, w_down_bf,
                               conv_hist[:, :d_ff], conv_hist[:, d_ff:], tm=n_s, seq_len=ts)
        outs[4].append(kf.reshape(bs, ts, N_HEADS, V_DIM))
        outs[5].append(vf.reshape(bs, ts, N_HEADS, V_DIM))
        outs[6].append(pool_state[:, POOL_HALO - POOL_BUF:, :])
        ffn_state = jnp.concatenate([fg, fv], axis=-1).reshape(bs, ts, 2 * d_ff)
        outs[7].append(ffn_state[:, ts - (CONV_W - 1):, :])

    return (yp, ys.reshape(bs, ts, d)) + tuple(jnp.stack(o) for o in outs)
```

```python
import functools
import math

import jax
import jax.numpy as jnp
from jax import lax
from jax.experimental import pallas as pl
from jax.experimental.pallas import tpu as pltpu

F32 = jnp.float32
BF16 = jnp.bfloat16

N_HEADS = 8
HEAD_DIM = 64
V_DIM = 2 * HEAD_DIM
POOL_WINDOWS = (2, 4, 8, 16)
POOL_GROUP = 256
POOL_BUF = max(POOL_WINDOWS) - 1
POOL_HALO = 16
CONV_W = 3
CONV_HALO = 8
PAGE_SIZE = 128
SUM_ROWS = 16
EPS = 1e-6
NEG = -1e30
Q_SCALE = HEAD_DIM ** -0.5 * math.log2(math.e)
FF_CHUNK = 256
FFN_ROW_PARTS = 2
IN_SECTION = 1024
IN_PART_ROWS = 256
VMEM_LIMIT = 56 * 1024 * 1024


def lambda_init_for(layer):
    return 0.8 - 0.6 * math.exp(-0.3 * layer)


def _params(*sem):
    return pltpu.CompilerParams(dimension_semantics=sem, vmem_limit_bytes=VMEM_LIMIT)


def _sigmoid(x):
    return 1.0 / (1.0 + jnp.exp(-x))


def _rms_rows(x, g):
    return x * lax.rsqrt(jnp.mean(x * x, axis=-1, keepdims=True) + EPS) * g


def _lambda_value(lam_ref, lam_init):
    lv = lam_ref[...]
    s1 = jnp.sum(lv[0:1] * lv[1:2], axis=-1, keepdims=True)
    s2 = jnp.sum(lv[2:3] * lv[3:4], axis=-1, keepdims=True)
    return jnp.exp(s1) - jnp.exp(s2) + lam_init


def _inproj_kernel(x_ref, nm_ref, w_ref, qg_ref, kg_ref,
                   q_ref, kf_ref, kb_ref, vf_ref, vb_ref, u_ref, sa_ref, sp_ref, h_ref, *, n_split):
    part = x_ref.shape[0] // n_split
    h_ref[...] = _rms_rows(x_ref[...], nm_ref[...]).astype(BF16)

    def section(j, epilogue):
        w = w_ref[:, j * IN_SECTION:(j + 1) * IN_SECTION]
        for r in range(n_split):
            rows = slice(r * part, (r + 1) * part)
            epilogue(rows, jnp.dot(h_ref[rows, :], w, preferred_element_type=F32))

    def group_norm(y, g_ref):
        r = lax.broadcasted_iota(jnp.int32, (V_DIM, V_DIM), 0) // HEAD_DIM
        c = lax.broadcasted_iota(jnp.int32, (V_DIM, V_DIM), 1) // HEAD_DIM
        ones_bd = (r == c).astype(BF16)
        parts = []
        for h in range(N_HEADS):
            yh = y[:, h * V_DIM:(h + 1) * V_DIM]
            ss = jnp.dot((yh * yh).astype(BF16), ones_bd, preferred_element_type=F32)
            parts.append(yh * lax.rsqrt(ss * (1.0 / HEAD_DIM) + EPS) * g_ref[...])
        return jnp.concatenate(parts, axis=-1)

    def store_heads(dst_ref, rows, val):
        for h in range(N_HEADS):
            dst_ref[rows, h, :] = val[:, h * V_DIM:(h + 1) * V_DIM]

    def q_epilogue(rows, y):
        q_ref[rows, :] = (group_norm(y, qg_ref) * Q_SCALE).astype(BF16)

    def k_epilogue(rows, y):
        kn = group_norm(y, kg_ref)
        store_heads(kf_ref, rows, kn)
        kb_ref[rows, :] = kn.astype(BF16)

    def v_epilogue(rows, y):
        store_heads(vf_ref, rows, y)
        vb_ref[:, rows] = y.T.astype(BF16)

    def u_epilogue(rows, y):
        u_ref[rows, :] = y

    def gate_epilogue(dst_ref):
        def epilogue(rows, y):
            dst_ref[rows, :] = _sigmoid(y).astype(BF16)
        return epilogue

    for j, epilogue in enumerate((q_epilogue, k_epilogue, v_epilogue, u_epilogue,
                                  gate_epilogue(sa_ref), gate_epilogue(sp_ref))):
        section(j, epilogue)


def _in_projection(x2d, norm_mix, w_in_bf, q_gain, k_gain, tm):
    n, d = x2d.shape
    row = lambda i: (i, 0)
    fixed = lambda i: (0, 0)
    tile = pl.BlockSpec((tm, IN_SECTION), row)
    heads_tile = pl.BlockSpec((tm, N_HEADS, V_DIM), lambda i: (i, 0, 0))
    flat = lambda dt: jax.ShapeDtypeStruct((n, IN_SECTION), dt)
    heads = jax.ShapeDtypeStruct((n, N_HEADS, V_DIM), F32)
    v_t_tile = pl.BlockSpec((IN_SECTION, tm), lambda i: (0, i))
    v_t = jax.ShapeDtypeStruct((IN_SECTION, n), BF16)
    out_specs = [tile, heads_tile, tile, heads_tile, v_t_tile, tile, tile, tile]
    out_shape = [flat(BF16), heads, flat(BF16), heads, v_t, flat(F32), flat(BF16), flat(BF16)]
    return pl.pallas_call(
        functools.partial(_inproj_kernel, n_split=max(1, tm // IN_PART_ROWS)),
        grid=(n // tm,),
        in_specs=[pl.BlockSpec((tm, d), row),
                  pl.BlockSpec((1, d), fixed),
                  pl.BlockSpec(w_in_bf.shape, fixed, pipeline_mode=pl.Buffered(1)),
                  pl.BlockSpec((1, V_DIM), fixed),
                  pl.BlockSpec((1, V_DIM), fixed)],
        out_specs=out_specs,
        out_shape=out_shape,
        scratch_shapes=[pltpu.VMEM((tm, d), BF16)],
        compiler_params=_params("arbitrary"),
        name="in_projection",
    )(x2d, norm_mix, w_in_bf, q_gain, k_gain)


def _prompt_attn_kernel(lam_ref, q_ref, k_ref, vt_ref, hn_ref, o_ref, m_ref, acc_ref, st0_ref, st1_ref,
                        *, blk_q, blk_k, n_hd, lam_init):
    i = pl.program_id(2)
    lane = lax.broadcasted_iota(jnp.int32, (blk_q, V_DIM), 1)
    q_comp = []
    for hd in range(n_hd):
        q = q_ref[0, :, hd * V_DIM:(hd + 1) * V_DIM]
        zero = jnp.zeros_like(q)
        q_comp.append((jnp.where(lane < HEAD_DIM, q, zero), jnp.where(lane >= HEAD_DIM, q, zero)))

    m_ref[...] = jnp.full(m_ref.shape, NEG, F32)
    acc_ref[...] = jnp.zeros(acc_ref.shape, F32)
    ones_rows = jnp.ones((SUM_ROWS, blk_k), BF16)

    def scores(j, st_ref, q_lo=0):
        start = pl.multiple_of(j * blk_k, blk_k)
        for hd in range(n_hd):
            k = k_ref[0, pl.ds(start, blk_k), hd * V_DIM:(hd + 1) * V_DIM]
            for c in range(2):
                st_ref[hd, c, :, q_lo:] = lax.dot_general(k, q_comp[hd][c][q_lo:], (((1,), (1,)), ((), ())),
                                                          preferred_element_type=F32)

    def update(j, st_ref, masked, q_lo=0):
        start = pl.multiple_of(j * blk_k, blk_k)
        for hd in range(n_hd):
            vt = jnp.concatenate([vt_ref[hd * V_DIM:(hd + 1) * V_DIM, pl.ds(start, blk_k)], ones_rows], axis=0)
            for c in range(2):
                st = st_ref[hd, c, :, q_lo:]
                if masked:
                    k_pos = start + lax.broadcasted_iota(jnp.int32, st.shape, 0)
                    q_pos = i * blk_q + q_lo + lax.broadcasted_iota(jnp.int32, st.shape, 1)
                    st = jnp.where(k_pos <= q_pos, st, NEG)
                m_old = m_ref[hd, c, :, q_lo:]
                m_new = jnp.maximum(m_old, jnp.max(st, axis=0, keepdims=True))
                pt = jnp.exp2(st - m_new)
                alpha = jnp.exp2(m_old - m_new)
                acc_ref[hd, c, :, q_lo:] = (alpha * acc_ref[hd, c, :, q_lo:]
                                            + jnp.dot(vt, pt.astype(BF16), preferred_element_type=F32))
                m_ref[hd, c, :, q_lo:] = m_new

    def body(jj, carry):
        scores(2 * jj + 1, st1_ref)
        update(2 * jj, st0_ref, False)
        scores(2 * jj + 2, st0_ref)
        update(2 * jj + 1, st1_ref, False)
        return carry

    assert blk_q == 2 * blk_k
    scores(0, st0_ref)
    lax.fori_loop(0, i, body, 0)
    scores(2 * i + 1, st1_ref, q_lo=blk_k)
    update(2 * i, st0_ref, True)
    update(2 * i + 1, st1_ref, True, q_lo=blk_k)

    lam = _lambda_value(lam_ref, lam_init)
    norm = lambda a: a[:V_DIM] / a[V_DIM:V_DIM + 1]
    for hd in range(n_hd):
        ot = norm(acc_ref[hd, 0]) - lam * norm(acc_ref[hd, 1])
        gain = hn_ref[hd * V_DIM:(hd + 1) * V_DIM, :]
        ot = ot * lax.rsqrt(jnp.mean(ot * ot, axis=0, keepdims=True) + EPS) * gain * (1.0 - lam_init)
        o_ref[0, :, hd * V_DIM:(hd + 1) * V_DIM] = ot.T.astype(BF16)


def _prompt_attention(lam_vecs, q, k, v_t, head_norm, lam_init, blk_q, blk_k, n_hd):
    b, t, _ = q.shape
    q_spec = pl.BlockSpec((1, blk_q, n_hd * V_DIM), lambda bi, h, i: (bi, i, h))
    return pl.pallas_call(
        functools.partial(_prompt_attn_kernel, blk_q=blk_q, blk_k=blk_k, n_hd=n_hd, lam_init=lam_init),
        grid=(b, N_HEADS // n_hd, t // blk_q),
        in_specs=[pl.BlockSpec(lam_vecs.shape, lambda bi, h, i: (0, 0)),
                  q_spec,
                  pl.BlockSpec((1, t, n_hd * V_DIM), lambda bi, h, i: (bi, 0, h)),
                  pl.BlockSpec((n_hd * V_DIM, t), lambda bi, h, i: (h, bi)),
                  pl.BlockSpec((n_hd * V_DIM, 1), lambda bi, h, i: (h, 0))],
        out_specs=q_spec,
        out_shape=jax.ShapeDtypeStruct(q.shape, BF16),
        scratch_shapes=[pltpu.VMEM((n_hd, 2, 1, blk_q), F32),
                        pltpu.VMEM((n_hd, 2, V_DIM + SUM_ROWS, blk_q), F32),
                        pltpu.VMEM((n_hd, 2, blk_k, blk_q), F32), pltpu.VMEM((n_hd, 2, blk_k, blk_q), F32)],
        compiler_params=_params("arbitrary", "arbitrary", "arbitrary"),
        name="prompt_attention",
    )(lam_vecs, q, k, v_t, head_norm.reshape(N_HEADS * V_DIM, 1))


def _decode_attn_kernel(pt_ref, lam_ref, q_ref, kn_ref, vn_ref, *refs, n_new, n_pg, lam_init):
    del pt_ref
    ck_refs, cv_refs = refs[:n_pg], refs[n_pg:2 * n_pg]
    hn_ref, o_ref, a_ref, m_ref, l_ref, acc_ref = refs[2 * n_pg:]
    p_idx = pl.program_id(1)
    n_cols = 2 * N_HEADS * n_new
    cols_per_head = 2 * n_new

    def head_of(shape, row_axis, col_axis):
        r = lax.broadcasted_iota(jnp.int32, shape, row_axis)
        c = lax.broadcasted_iota(jnp.int32, shape, col_axis)
        return r, c, c // cols_per_head

    def update(blocks, causal):
        h_idx, c_idx, c_head = head_of((N_HEADS, n_cols), 0, 1)
        own_head = h_idx == c_head
        raw = [lax.dot_general(k2d, a_ref[...], (((1,), (1,)), ((), ())), preferred_element_type=F32)
               for k2d, _ in blocks]
        scores, valids = [], []
        for s in raw:
            n_tok = s.shape[0] // N_HEADS
            s3 = s.reshape(n_tok, N_HEADS, n_cols)
            valid = own_head[None]
            if causal:
                t_idx = lax.broadcasted_iota(jnp.int32, s3.shape, 0)
                q_idx = lax.broadcasted_iota(jnp.int32, s3.shape, 2) % n_new
                valid = valid & (t_idx <= q_idx)
            scores.append(s3)
            valids.append(valid)
        m_old = m_ref[...]
        m_new = m_old
        for s3, valid in zip(scores, valids):
            m_new = jnp.maximum(m_new, jnp.max(jnp.where(valid, s3, NEG), axis=0))
        alpha = jnp.exp2(m_old - m_new)
        l_new = alpha * l_ref[...]
        pv = None
        for (_, v2d), s3, valid in zip(blocks, scores, valids):
            p3 = jnp.where(valid, jnp.exp2(s3 - m_new[None]), 0.0)
            l_new = l_new + jnp.sum(p3, axis=0)
            part = lax.dot_general(v2d, p3.reshape(v2d.shape[0], n_cols), (((0,), (0,)), ((), ())),
                                   preferred_element_type=F32)
            pv = part if pv is None else pv + part
        l_ref[...] = l_new
        m_ref[...] = m_new
        alpha_row = jnp.sum(jnp.where(own_head, alpha, 0.0), axis=0, keepdims=True)
        acc_ref[...] = acc_ref[...] * alpha_row + pv

    @pl.when(p_idx == 0)
    def _():
        q = q_ref[0].astype(F32)
        lane = lax.broadcasted_iota(jnp.int32, (n_new, V_DIM), 1)
        rows = []
        for h in range(N_HEADS):
            qh = q[:, h * V_DIM:(h + 1) * V_DIM]
            rows += [jnp.where(lane < HEAD_DIM, qh, 0.0), jnp.where(lane >= HEAD_DIM, qh, 0.0)]
        a_ref[...] = jnp.concatenate(rows, axis=0)
        m_ref[...] = jnp.full(m_ref.shape, NEG, F32)
        l_ref[...] = jnp.zeros(l_ref.shape, F32)
        acc_ref[...] = jnp.zeros(acc_ref.shape, F32)
        update([(kn_ref[0], vn_ref[0])], True)

    rows = PAGE_SIZE * N_HEADS
    pages = [(ck[0].reshape(rows, V_DIM), cv[0].reshape(rows, V_DIM)) for ck, cv in zip(ck_refs, cv_refs)]
    update(pages, False)

    @pl.when(p_idx == pl.num_programs(1) - 1)
    def _():
        lam = _lambda_value(lam_ref, lam_init)
        h_idx, c_idx, c_head = head_of((N_HEADS, n_cols), 0, 1)
        l_row = jnp.sum(jnp.where(h_idx == c_head, l_ref[...], 0.0), axis=0, keepdims=True)
        on = (acc_ref[...] / l_row).T
        parts = []
        for h in range(N_HEADS):
            blk = on[h * cols_per_head:(h + 1) * cols_per_head]
            o = blk[:n_new] - lam * blk[n_new:]
            parts.append(_rms_rows(o, hn_ref[:, h * V_DIM:(h + 1) * V_DIM]) * (1.0 - lam_init))
        o_ref[0] = jnp.concatenate(parts, axis=-1).astype(BF16)


def _decode_attention(page_table, lam_vecs, q, k_new, v_new, cache_k, cache_v, head_norm, lam_init):
    b, n_new, width = q.shape
    n_pages = page_table.shape[1]
    n_cols = 2 * N_HEADS * n_new
    seq_spec = pl.BlockSpec((1, n_new, width), lambda bi, p, pt: (bi, 0, 0))
    new_spec = pl.BlockSpec((1, n_new * N_HEADS, V_DIM), lambda bi, p, pt: (bi, 0, 0))
    n_pg = DECODE_PAGES_PER_STEP
    page_specs = [pl.BlockSpec((1, PAGE_SIZE, N_HEADS, V_DIM),
                               lambda bi, p, pt, s=s: (pt[bi, p * n_pg + s], 0, 0, 0)) for s in range(n_pg)]
    grid_spec = pltpu.PrefetchScalarGridSpec(
        num_scalar_prefetch=1,
        grid=(b, n_pages // n_pg),
        in_specs=[pl.BlockSpec(lam_vecs.shape, lambda bi, p, pt: (0, 0)),
                  seq_spec, new_spec, new_spec, *page_specs, *page_specs,
                  pl.BlockSpec((1, width), lambda bi, p, pt: (0, 0))],
        out_specs=seq_spec,
        scratch_shapes=[pltpu.VMEM((n_cols, V_DIM), F32), pltpu.VMEM((N_HEADS, n_cols), F32),
                        pltpu.VMEM((N_HEADS, n_cols), F32), pltpu.VMEM((V_DIM, n_cols), F32)],
    )
    return pl.pallas_call(
        functools.partial(_decode_attn_kernel, n_new=n_new, n_pg=n_pg, lam_init=lam_init),
        grid_spec=grid_spec,
        out_shape=jax.ShapeDtypeStruct(q.shape, BF16),
        compiler_params=_params("arbitrary", "arbitrary"),
        name="decode_attention",
    )(page_table, lam_vecs, q, k_new, v_new, *([cache_k] * n_pg), *([cache_v] * n_pg),
      head_norm.reshape(1, width))


def _pool_branch(ext, n_seq, seq_len, pos, wp_ref, ps_ref):
    rows = POOL_HALO + seq_len

    def tokens(a):
        if n_seq == 1:
            return a[POOL_HALO:]
        return a.reshape(n_seq, rows, a.shape[-1])[:, POOL_HALO:, :].reshape(n_seq * seq_len, a.shape[-1])

    cur = ext
    outs = []
    for g, w in enumerate(POOL_WINDOWS):
        cur = cur + pltpu.roll(cur, w // 2, axis=0)
        c0 = g * POOL_GROUP
        win = tokens(cur[:, :POOL_GROUP])
        tok = tokens(ext[:, c0:c0 + POOL_GROUP])
        cnt = jnp.minimum(w, pos + 1).astype(F32)
        d = (win / cnt - tok).astype(BF16)
        outs.append(jnp.dot(d, wp_ref[g], preferred_element_type=F32) * ps_ref[:, c0:c0 + POOL_GROUP])
        cur = cur[:, POOL_GROUP:]
    return jnp.concatenate(outs, axis=-1)


def _merge_project(x, o_n, pool_o, sig_a, sig_p, wa_ref, wpb_ref, wo_ref):
    attn = jnp.dot(o_n, wa_ref[...], preferred_element_type=F32)
    pool = jnp.dot(pool_o.astype(BF16), wpb_ref[...], preferred_element_type=F32)
    merged = sig_a * attn + sig_p * pool
    return x + jnp.dot(merged.astype(BF16), wo_ref[...], preferred_element_type=F32)


def _prompt_mix_kernel(u_ref, halo_ref, on_ref, sa_ref, sp_ref, x_ref, wp_ref, ps_ref, wa_ref, wpb_ref, wo_ref,
                       x1_ref, *, tm):
    i = pl.program_id(1)
    halo = jnp.where(i > 0, halo_ref[0], 0.0)
    ext = jnp.concatenate([halo, u_ref[0]], axis=0)
    pos = i * tm + lax.broadcasted_iota(jnp.int32, (tm, 1), 0)
    pool_o = _pool_branch(ext, 1, tm, pos, wp_ref, ps_ref)
    x1_ref[0] = _merge_project(x_ref[0], on_ref[0], pool_o, sa_ref[0], sp_ref[0], wa_ref, wpb_ref, wo_ref)


def _prompt_mix(u, o_n, sig_a, sig_p, x, w_pool, pool_scale, w_attn, w_poolb, w_out, tm):
    b, t, d = x.shape
    tile = lambda bi, i: (bi, i, 0)
    fixed2 = lambda bi, i: (0, 0)
    halo_blocks = tm // POOL_HALO
    tile_spec = pl.BlockSpec((1, tm, d), tile)
    return pl.pallas_call(
        functools.partial(_prompt_mix_kernel, tm=tm),
        grid=(b, t // tm),
        in_specs=[tile_spec,
                  pl.BlockSpec((1, POOL_HALO, d), lambda bi, i: (bi, jnp.maximum(i * halo_blocks - 1, 0), 0)),
                  tile_spec, tile_spec, tile_spec, tile_spec,
                  pl.BlockSpec(w_pool.shape, lambda bi, i: (0, 0, 0)),
                  pl.BlockSpec((1, d), fixed2),
                  pl.BlockSpec(w_attn.shape, fixed2),
                  pl.BlockSpec(w_poolb.shape, fixed2),
                  pl.BlockSpec(w_out.shape, fixed2)],
        out_specs=tile_spec,
        out_shape=jax.ShapeDtypeStruct(x.shape, F32),
        compiler_params=_params("arbitrary", "arbitrary"),
        name="prompt_mix",
    )(u, u, o_n, sig_a, sig_p, x, w_pool, pool_scale, w_attn, w_poolb, w_out)


def _sample_mix_kernel(u_ref, hist_ref, on_ref, sa_ref, sp_ref, x_ref, wp_ref, ps_ref, wa_ref, wpb_ref, wo_ref,
                       x1_ref, state_ref, *, n_seq, seq_len, pos0):
    d = u_ref.shape[-1]
    rows = POOL_HALO + seq_len
    ext3 = jnp.concatenate([hist_ref[...], u_ref[...].reshape(n_seq, seq_len, d)], axis=1)
    state_ref[...] = ext3[:, rows - POOL_HALO:, :]
    pos = pos0 + lax.broadcasted_iota(jnp.int32, (n_seq * seq_len, 1), 0) % seq_len
    pool_o = _pool_branch(ext3.reshape(n_seq * rows, d), n_seq, seq_len, pos, wp_ref, ps_ref)
    x1_ref[...] = _merge_project(x_ref[...], on_ref[...], pool_o, sa_ref[...], sp_ref[...], wa_ref, wpb_ref, wo_ref)


def _sample_mix(u, hist, o_n, sig_a, sig_p, x, w_pool, pool_scale, w_attn, w_poolb, w_out, seq_len, pos0):
    n, d = x.shape
    n_seq = n // seq_len
    full = lambda a: pl.BlockSpec(a.shape, lambda i, nd=a.ndim: (0,) * nd)
    args = (u, hist, o_n, sig_a, sig_p, x, w_pool, pool_scale, w_attn, w_poolb, w_out)
    out_shape = [jax.ShapeDtypeStruct(x.shape, F32), jax.ShapeDtypeStruct(hist.shape, F32)]
    return pl.pallas_call(
        functools.partial(_sample_mix_kernel, n_seq=n_seq, seq_len=seq_len, pos0=pos0),
        grid=(1,),
        in_specs=[full(a) for a in args],
        out_specs=[full(s) for s in out_shape],
        out_shape=out_shape,
        compiler_params=_params("arbitrary"),
        name="sample_mix",
    )(*args)


def _conv_taps(up, hist, n_seq, seq_len):
    c = up.shape[-1]
    rows = CONV_HALO + seq_len
    if n_seq == 1:
        ext = jnp.concatenate([hist, up], axis=0)
    else:
        ext = jnp.concatenate([hist.reshape(n_seq, CONV_HALO, c), up.reshape(n_seq, seq_len, c)], axis=1)
        ext = ext.reshape(n_seq * rows, c)

    def tokens(a):
        if n_seq == 1:
            return a[CONV_HALO:]
        return a.reshape(n_seq, rows, c)[:, CONV_HALO:, :].reshape(n_seq * seq_len, c)

    return tokens(pltpu.roll(ext, 1, axis=0)), tokens(pltpu.roll(ext, 2, axis=0))


def _ffn_kernel(x1_ref, nf_ref, wg_ref, wv_ref, cwg_ref, cwv_ref, cbg_ref, cbv_ref, wd_ref, hg_ref, hv_ref,
                y_ref, sg_ref, sv_ref, h_ref, acc_ref, *, n_seq, seq_len):
    j = pl.program_id(1)
    n_j = pl.num_programs(1)

    @pl.when(j == 0)
    def _():
        h_ref[...] = _rms_rows(x1_ref[...], nf_ref[...]).astype(BF16)
        acc_ref[...] = jnp.zeros(acc_ref.shape, F32)

    h = h_ref[...]
    halves = []
    for w_ref, cw_ref, cb_ref, hist_ref, s_ref in ((wg_ref, cwg_ref, cbg_ref, hg_ref, sg_ref),
                                                   (wv_ref, cwv_ref, cbv_ref, hv_ref, sv_ref)):
        up = jnp.dot(h, w_ref[...], preferred_element_type=F32)
        s_ref[...] = up
        prev1, prev2 = _conv_taps(up, hist_ref[...], n_seq, seq_len)
        cw = cw_ref[...]
        halves.append(cb_ref[...] + prev2 * cw[0:1] + prev1 * cw[1:2] + up * cw[2:3])
    gate, val = halves
    act = (gate * _sigmoid(gate) * val).astype(BF16)
    acc_ref[...] += jnp.dot(act, wd_ref[...], preferred_element_type=F32)

    @pl.when(j == n_j - 1)
    def _():
        y_ref[...] = x1_ref[...] + acc_ref[...]


def _conv_ffn(x1, norm_ffn, w_up, conv_w, conv_b, w_down, hist_gate, hist_val, tm, seq_len):
    n, d = x1.shape
    d_ff = w_down.shape[0]
    n_chunks = d_ff // FF_CHUNK
    n_seq = tm // seq_len
    row = lambda i, j: (i, 0)
    gate_col = lambda i, j: (0, j)
    val_col = lambda i, j: (0, n_chunks + j)
    up_shape = jax.ShapeDtypeStruct((n, d_ff), F32)
    up_spec = pl.BlockSpec((tm, FF_CHUNK), lambda i, j: (i, j))
    hist_spec = pl.BlockSpec((n_seq * CONV_HALO, FF_CHUNK), lambda i, j: (i, j))
    return pl.pallas_call(
        functools.partial(_ffn_kernel, n_seq=n_seq, seq_len=seq_len),
        grid=(n // tm, n_chunks),
        in_specs=[pl.BlockSpec((tm, d), row),
                  pl.BlockSpec((1, d), lambda i, j: (0, 0)),
                  pl.BlockSpec((d, FF_CHUNK), gate_col),
                  pl.BlockSpec((d, FF_CHUNK), val_col),
                  pl.BlockSpec((CONV_W, FF_CHUNK), gate_col),
                  pl.BlockSpec((CONV_W, FF_CHUNK), val_col),
                  pl.BlockSpec((1, FF_CHUNK), gate_col),
                  pl.BlockSpec((1, FF_CHUNK), val_col),
                  pl.BlockSpec((FF_CHUNK, d), lambda i, j: (j, 0)),
                  hist_spec, hist_spec],
        out_specs=[pl.BlockSpec((tm, d), row), up_spec, up_spec],
        out_shape=[jax.ShapeDtypeStruct((n, d), F32), up_shape, up_shape],
        scratch_shapes=[pltpu.VMEM((tm, d), BF16), pltpu.VMEM((tm, d), F32)],
        compiler_params=_params("arbitrary", "arbitrary"),
        name="conv_ffn",
    )(x1, norm_ffn, w_up, w_up, conv_w, conv_w, conv_b, conv_b, w_down, hist_gate, hist_val)


def _ffn_seq_kernel(x1_ref, nf_ref, wu_ref, cw_ref, cb_ref, wd_ref, y_ref, state_ref,
                    h_ref, acc_ref, carry_ref, up0_ref, up1_ref):
    i = pl.program_id(1)
    tm = x1_ref.shape[0]
    d_ff = wd_ref.shape[0]
    n_chunks = d_ff // FF_CHUNK
    h_ref[...] = _rms_rows(x1_ref[...], nf_ref[...]).astype(BF16)
    acc_ref[...] = jnp.zeros(acc_ref.shape, F32)
    up_refs = (up0_ref, up1_ref)

    def cols(c, half):
        return slice(half * d_ff + c * FF_CHUNK, half * d_ff + (c + 1) * FF_CHUNK)

    def up_project(c):
        for half in range(2):
            up_refs[c % 2][half] = jnp.dot(h_ref[...], wu_ref[:, cols(c, half)], preferred_element_type=F32)

    def consume(c):
        part = tm // FFN_ROW_PARTS
        hists = [jnp.where(i > 0, carry_ref[:, cols(c, half)], 0.0) for half in range(2)]
        for r in range(FFN_ROW_PARTS):
            rows = slice(r * part, (r + 1) * part)
            halves = []
            for half in range(2):
                up = up_refs[c % 2][half, rows, :]
                prev1, prev2 = _conv_taps(up, hists[half], 1, part)
                hists[half] = up[part - CONV_HALO:]
                cw = cw_ref[:, cols(c, half)]
                halves.append(cb_ref[:, cols(c, half)] + prev2 * cw[0:1] + prev1 * cw[1:2] + up * cw[2:3])
            gate, val = halves
            act = (gate * _sigmoid(gate) * val).astype(BF16)
            acc_ref[rows, :] += jnp.dot(act, wd_ref[c * FF_CHUNK:(c + 1) * FF_CHUNK, :],
                                        preferred_element_type=F32)
        for half in range(2):
            carry_ref[:, cols(c, half)] = hists[half]
            state_ref[0, :, cols(c, half)] = hists[half]

    up_project(0)
    for c in range(n_chunks):
        if c + 1 < n_chunks:
            up_project(c + 1)
        consume(c)
    y_ref[...] = x1_ref[...] + acc_ref[...]


def _conv_ffn_seq(x1, norm_ffn, w_up, conv_w, conv_b, w_down, n_batch, tm):
    n, d = x1.shape
    d_ff = w_down.shape[0]
    tiles = n // (n_batch * tm)
    row = lambda bi, i: (bi * tiles + i, 0)
    whole = lambda a: pl.BlockSpec(a.shape, lambda bi, i, nd=a.ndim: (0,) * nd, pipeline_mode=pl.Buffered(1))
    state_shape = (n_batch * tiles, CONV_HALO, 2 * d_ff)
    return pl.pallas_call(
        _ffn_seq_kernel,
        grid=(n_batch, tiles),
        in_specs=[pl.BlockSpec((tm, d), row), whole(norm_ffn), whole(w_up), whole(conv_w), whole(conv_b),
                  whole(w_down)],
        out_specs=[pl.BlockSpec((tm, d), row),
                   pl.BlockSpec((1,) + state_shape[1:], lambda bi, i: (bi * tiles + i, 0, 0))],
        out_shape=[jax.ShapeDtypeStruct((n, d), F32), jax.ShapeDtypeStruct(state_shape, F32)],
        scratch_shapes=[pltpu.VMEM((tm, d), BF16), pltpu.VMEM((tm, d), F32),
                        pltpu.VMEM((CONV_HALO, 2 * d_ff), F32),
                        pltpu.VMEM((2, tm, FF_CHUNK), F32), pltpu.VMEM((2, tm, FF_CHUNK), F32)],
        compiler_params=_params("arbitrary", "arbitrary"),
        name="conv_ffn_seq",
    )(x1, norm_ffn, w_up, conv_w, conv_b, w_down)


PROMPT_TM = 512
FFN_TM = 512
DECODE_PAGES_PER_STEP = 8
ATTN_BLK_Q = 1024
ATTN_BLK_K = 512
ATTN_HEADS_PER_STEP = 2


def kernel(x_prompt, x_sample, cache_k, cache_v, state_pool, state_ffn_conv, page_table, norm_mix, w_in, q_norm, k_norm, lambda_q1, lambda_k1, lambda_q2, lambda_k2, head_norm, w_pool, pool_scale, w_attn_branch, w_pool_branch, w_out, norm_ffn, w_up, conv_w, conv_b, w_down):
    depth = w_in.shape[0]
    b, t, d = x_prompt.shape
    bs, ts, _ = x_sample.shape
    width = N_HEADS * V_DIM
    d_ff = w_down.shape[1]
    past_len = page_table.shape[1] * PAGE_SIZE

    yp, ys = x_prompt, x_sample.reshape(bs * ts, d)
    outs = [[] for _ in range(8)]
    for l in range(depth):
        lam_init = lambda_init_for(l)
        w_in_bf = w_in[l].astype(BF16)
        w_pool_bf = w_pool[l].astype(BF16)
        w_attn_bf = w_attn_branch[l].astype(BF16)
        w_poolb_bf = w_pool_branch[l].astype(BF16)
        w_out_bf = w_out[l].astype(BF16)
        w_up_bf = w_up[l].astype(BF16)
        w_down_bf = w_down[l].astype(BF16)
        nm = norm_mix[l].reshape(1, d)
        nf = norm_ffn[l].reshape(1, d)
        qg = q_norm[l].reshape(1, V_DIM)
        kg = k_norm[l].reshape(1, V_DIM)
        lam_vecs = jnp.stack([lambda_q1[l], lambda_k1[l], lambda_q2[l], lambda_k2[l]])
        ps = pool_scale[l].reshape(1, -1)
        cb = conv_b[l].reshape(1, -1)

        q, kf, kb, vf, vb, u, sa, sp = _in_projection(yp.reshape(b * t, d), nm, w_in_bf, qg, kg, PROMPT_TM)
        r3 = lambda a: a.reshape(b, t, -1)
        o_n = _prompt_attention(lam_vecs, r3(q), r3(kb), vb, head_norm[l], lam_init,
                                ATTN_BLK_Q, ATTN_BLK_K, ATTN_HEADS_PER_STEP)
        x1 = _prompt_mix(r3(u), o_n, r3(sa), r3(sp), yp, w_pool_bf, ps, w_attn_bf, w_poolb_bf, w_out_bf, PROMPT_TM)
        y2d, f_tails = _conv_ffn_seq(x1.reshape(b * t, d), nf, w_up_bf, conv_w[l], cb, w_down_bf, b, FFN_TM)
        yp = y2d.reshape(b, t, d)
        outs[0].append(kf.reshape(b, t, N_HEADS, V_DIM))
        outs[1].append(vf.reshape(b, t, N_HEADS, V_DIM))
        outs[2].append(r3(u)[:, t - POOL_BUF:, :])
        last = f_tails.reshape((b, -1) + f_tails.shape[1:])[:, -1]
        outs[3].append(last[:, CONV_HALO - (CONV_W - 1):, :])

        n_s = bs * ts
        q, kf, kb, vf, vb, u, sa, sp = _in_projection(ys, nm, w_in_bf, qg, kg, n_s)
        s3 = lambda a: a.reshape(bs, ts, -1)
        o_n = _decode_attention(page_table, lam_vecs, s3(q),
                                kf.reshape(bs, ts * N_HEADS, V_DIM), vf.reshape(bs, ts * N_HEADS, V_DIM),
                                cache_k[l], cache_v[l], head_norm[l], lam_init)
        pool_hist = jnp.pad(state_pool[l], ((0, 0), (POOL_HALO - POOL_BUF, 0), (0, 0)))
        x1, pool_state = _sample_mix(u, pool_hist, o_n.reshape(n_s, width), sa, sp, ys, w_pool_bf, ps,
                                     w_attn_bf, w_poolb_bf, w_out_bf, ts, past_len)
        conv_hist = jnp.pad(state_ffn_conv[l], ((0, 0), (CONV_HALO - (CONV_W - 1), 0), (0, 0)))
        conv_hist = conv_hist.reshape(bs * CONV_HALO, 2 * d_ff)
        ys, fg, fv = _conv_ffn(x1, nf, w_up_bf, conv_w[l], cb, w_down_bf,
                               conv_hist[:, :d_ff], conv_hist[:, d_ff:], tm=n_s, seq_len=ts)
        outs[4].append(kf.reshape(bs, ts, N_HEADS, V_DIM))
        outs[5].append(vf.reshape(bs, ts, N_HEADS, V_DIM))
        outs[6].append(pool_state[:, POOL_HALO - POOL_BUF:, :])
        ffn_state = jnp.concatenate([fg, fv], axis=-1).reshape(bs, ts, 2 * d_ff)
        outs[7].append(ffn_state[:, ts - (CONV_W - 1):, :])

    return (yp, ys.reshape(bs, ts, d)) + tuple(jnp.stack(o) for o in outs)
```

```python
import functools
import math

import jax
import jax.numpy as jnp
from jax import lax
from jax.experimental import pallas as pl
from jax.experimental.pallas import tpu as pltpu

F32 = jnp.float32
BF16 = jnp.bfloat16

N_HEADS = 8
HEAD_DIM = 64
V_DIM = 2 * HEAD_DIM
POOL_WINDOWS = (2, 4, 8, 16)
POOL_GROUP = 256
POOL_BUF = max(POOL_WINDOWS) - 1
POOL_HALO = 16
CONV_W = 3
CONV_HALO = 8
PAGE_SIZE = 128
SUM_ROWS = 16
EPS = 1e-6
NEG = -1e30
Q_SCALE = HEAD_DIM ** -0.5 * math.log2(math.e)
FF_CHUNK = 256
FFN_ROW_PARTS = 2
IN_SECTION = 1024
IN_PART_ROWS = 256
VMEM_LIMIT = 56 * 1024 * 1024


def lambda_init_for(layer):
    return 0.8 - 0.6 * math.exp(-0.3 * layer)


def _params(*sem):
    return pltpu.CompilerParams(dimension_semantics=sem, vmem_limit_bytes=VMEM_LIMIT)


def _sigmoid(x):
    return 1.0 / (1.0 + jnp.exp(-x))


def _rms_rows(x, g):
    return x * lax.rsqrt(jnp.mean(x * x, axis=-1, keepdims=True) + EPS) * g


def _lambda_value(lam_ref, lam_init):
    lv = lam_ref[...]
    s1 = jnp.sum(lv[0:1] * lv[1:2], axis=-1, keepdims=True)
    s2 = jnp.sum(lv[2:3] * lv[3:4], axis=-1, keepdims=True)
    return jnp.exp(s1) - jnp.exp(s2) + lam_init


def _inproj_kernel(x_ref, nm_ref, w_ref, qg_ref, kg_ref,
                   q_ref, kf_ref, kb_ref, vf_ref, vb_ref, u_ref, sa_ref, sp_ref, h_ref, *, n_split):
    part = x_ref.shape[0] // n_split
    h_ref[...] = _rms_rows(x_ref[...], nm_ref[...]).astype(BF16)

    def section(j, epilogue):
        w = w_ref[:, j * IN_SECTION:(j + 1) * IN_SECTION]
        for r in range(n_split):
            rows = slice(r * part, (r + 1) * part)
            epilogue(rows, jnp.dot(h_ref[rows, :], w, preferred_element_type=F32))

    def group_norm(y, g_ref):
        r = lax.broadcasted_iota(jnp.int32, (V_DIM, V_DIM), 0) // HEAD_DIM
        c = lax.broadcasted_iota(jnp.int32, (V_DIM, V_DIM), 1) // HEAD_DIM
        ones_bd = (r == c).astype(BF16)
        parts = []
        for h in range(N_HEADS):
            yh = y[:, h * V_DIM:(h + 1) * V_DIM]
            ss = jnp.dot((yh * yh).astype(BF16), ones_bd, preferred_element_type=F32)
            parts.append(yh * lax.rsqrt(ss * (1.0 / HEAD_DIM) + EPS) * g_ref[...])
        return jnp.concatenate(parts, axis=-1)

    def store_heads(dst_ref, rows, val):
        for h in range(N_HEADS):
            dst_ref[rows, h, :] = val[:, h * V_DIM:(h + 1) * V_DIM]

    def q_epilogue(rows, y):
        q_ref[rows, :] = (group_norm(y, qg_ref) * Q_SCALE).astype(BF16)

    def k_epilogue(rows, y):
        kn = group_norm(y, kg_ref)
        store_heads(kf_ref, rows, kn)
        kb_ref[rows, :] = kn.astype(BF16)

    def v_epilogue(rows, y):
        store_heads(vf_ref, rows, y)
        vb_ref[:, rows] = y.T.astype(BF16)

    def u_epilogue(rows, y):
        u_ref[rows, :] = y

    def gate_epilogue(dst_ref):
        def epilogue(rows, y):
            dst_ref[rows, :] = _sigmoid(y).astype(BF16)
        return epilogue

    for j, epilogue in enumerate((q_epilogue, k_epilogue, v_epilogue, u_epilogue,
                                  gate_epilogue(sa_ref), gate_epilogue(sp_ref))):
        section(j, epilogue)


def _in_projection(x2d, norm_mix, w_in_bf, q_gain, k_gain, tm):
    n, d = x2d.shape
    row = lambda i: (i, 0)
    fixed = lambda i: (0, 0)
    tile = pl.BlockSpec((tm, IN_SECTION), row)
    heads_tile = pl.BlockSpec((tm, N_HEADS, V_DIM), lambda i: (i, 0, 0))
    flat = lambda dt: jax.ShapeDtypeStruct((n, IN_SECTION), dt)
    heads = jax.ShapeDtypeStruct((n, N_HEADS, V_DIM), F32)
    v_t_tile = pl.BlockSpec((IN_SECTION, tm), lambda i: (0, i))
    v_t = jax.ShapeDtypeStruct((IN_SECTION, n), BF16)
    out_specs = [tile, heads_tile, tile, heads_tile, v_t_tile, tile, tile, tile]
    out_shape = [flat(BF16), heads, flat(BF16), heads, v_t, flat(F32), flat(BF16), flat(BF16)]
    return pl.pallas_call(
        functools.partial(_inproj_kernel, n_split=max(1, tm // IN_PART_ROWS)),
        grid=(n // tm,),
        in_specs=[pl.BlockSpec((tm, d), row),
                  pl.BlockSpec((1, d), fixed),
                  pl.BlockSpec(w_in_bf.shape, fixed, pipeline_mode=pl.Buffered(1)),
                  pl.BlockSpec((1, V_DIM), fixed),
                  pl.BlockSpec((1, V_DIM), fixed)],
        out_specs=out_specs,
        out_shape=out_shape,
        scratch_shapes=[pltpu.VMEM((tm, d), BF16)],
        compiler_params=_params("arbitrary"),
        name="in_projection",
    )(x2d, norm_mix, w_in_bf, q_gain, k_gain)


def _prompt_attn_kernel(lam_ref, q_ref, k_ref, vt_ref, hn_ref, o_ref, m_ref, acc_ref, st0_ref, st1_ref,
                        *, blk_q, blk_k, n_hd, lam_init):
    i = pl.program_id(2)
    lane = lax.broadcasted_iota(jnp.int32, (blk_q, V_DIM), 1)
    q_comp = []
    for hd in range(n_hd):
        q = q_ref[0, :, hd * V_DIM:(hd + 1) * V_DIM]
        zero = jnp.zeros_like(q)
        q_comp.append((jnp.where(lane < HEAD_DIM, q, zero), jnp.where(lane >= HEAD_DIM, q, zero)))

    m_ref[...] = jnp.full(m_ref.shape, NEG, F32)
    acc_ref[...] = jnp.zeros(acc_ref.shape, F32)
    ones_rows = jnp.ones((SUM_ROWS, blk_k), BF16)

    def scores(j, st_ref, q_lo=0):
        start = pl.multiple_of(j * blk_k, blk_k)
        for hd in range(n_hd):
            k = k_ref[0, pl.ds(start, blk_k), hd * V_DIM:(hd + 1) * V_DIM]
            for c in range(2):
                st_ref[hd, c, :, q_lo:] = lax.dot_general(k, q_comp[hd][c][q_lo:], (((1,), (1,)), ((), ())),
                                                          preferred_element_type=F32)

    def update(j, st_ref, masked, q_lo=0):
        start = pl.multiple_of(j * blk_k, blk_k)
        for hd in range(n_hd):
            vt = jnp.concatenate([vt_ref[hd * V_DIM:(hd + 1) * V_DIM, pl.ds(start, blk_k)], ones_rows], axis=0)
            for c in range(2):
                st = st_ref[hd, c, :, q_lo:]
                if masked:
                    k_idx = lax.broadcasted_iota(jnp.int32, (blk_k, blk_k), 0)
                    q_idx = lax.broadcasted_iota(jnp.int32, (blk_k, blk_k), 1)
                    causal = jnp.where(k_idx <= q_idx, st[:, :blk_k], NEG)
                    st = causal if st.shape[1] == blk_k else jnp.concatenate([causal, st[:, blk_k:]], axis=1)
                m_old = m_ref[hd, c, :, q_lo:]
                m_new = jnp.maximum(m_old, jnp.max(st, axis=0, keepdims=True))
                pt = jnp.exp2(st - m_new)
                alpha = jnp.exp2(m_old - m_new)
                acc_ref[hd, c, :, q_lo:] = (alpha * acc_ref[hd, c, :, q_lo:]
                                            + jnp.dot(vt, pt.astype(BF16), preferred_element_type=F32))
                m_ref[hd, c, :, q_lo:] = m_new

    def body(jj, carry):
        scores(2 * jj + 1, st1_ref)
        update(2 * jj, st0_ref, False)
        scores(2 * jj + 2, st0_ref)
        update(2 * jj + 1, st1_ref, False)
        return carry

    assert blk_q == 2 * blk_k
    scores(0, st0_ref)
    lax.fori_loop(0, i, body, 0)
    scores(2 * i + 1, st1_ref, q_lo=blk_k)
    update(2 * i, st0_ref, True)
    update(2 * i + 1, st1_ref, True, q_lo=blk_k)

    lam = _lambda_value(lam_ref, lam_init)
    norm = lambda a: a[:V_DIM] / a[V_DIM:V_DIM + 1]
    for hd in range(n_hd):
        ot = norm(acc_ref[hd, 0]) - lam * norm(acc_ref[hd, 1])
        gain = hn_ref[hd * V_DIM:(hd + 1) * V_DIM, :]
        ot = ot * lax.rsqrt(jnp.mean(ot * ot, axis=0, keepdims=True) + EPS) * gain * (1.0 - lam_init)
        o_ref[0, :, hd * V_DIM:(hd + 1) * V_DIM] = ot.T.astype(BF16)


def _prompt_attention(lam_vecs, q, k, v_t, head_norm, lam_init, blk_q, blk_k, n_hd):
    b, t, _ = q.shape
    q_spec = pl.BlockSpec((1, blk_q, n_hd * V_DIM), lambda bi, h, i: (bi, i, h))
    return pl.pallas_call(
        functools.partial(_prompt_attn_kernel, blk_q=blk_q, blk_k=blk_k, n_hd=n_hd, lam_init=lam_init),
        grid=(b, N_HEADS // n_hd, t // blk_q),
        in_specs=[pl.BlockSpec(lam_vecs.shape, lambda bi, h, i: (0, 0)),
                  q_spec,
                  pl.BlockSpec((1, t, n_hd * V_DIM), lambda bi, h, i: (bi, 0, h)),
                  pl.BlockSpec((n_hd * V_DIM, t), lambda bi, h, i: (h, bi)),
                  pl.BlockSpec((n_hd * V_DIM, 1), lambda bi, h, i: (h, 0))],
        out_specs=q_spec,
        out_shape=jax.ShapeDtypeStruct(q.shape, BF16),
        scratch_shapes=[pltpu.VMEM((n_hd, 2, 1, blk_q), F32),
                        pltpu.VMEM((n_hd, 2, V_DIM + SUM_ROWS, blk_q), F32),
                        pltpu.VMEM((n_hd, 2, blk_k, blk_q), F32), pltpu.VMEM((n_hd, 2, blk_k, blk_q), F32)],
        compiler_params=_params("arbitrary", "arbitrary", "arbitrary"),
        name="prompt_attention",
    )(lam_vecs, q, k, v_t, head_norm.reshape(N_HEADS * V_DIM, 1))


def _decode_attn_kernel(pt_ref, lam_ref, q_ref, kn_ref, vn_ref, *refs, n_new, n_pg, lam_init):
    del pt_ref
    ck_refs, cv_refs = refs[:n_pg], refs[n_pg:2 * n_pg]
    hn_ref, o_ref, a_ref, m_ref, l_ref, acc_ref = refs[2 * n_pg:]
    p_idx = pl.program_id(1)
    n_cols = 2 * N_HEADS * n_new
    cols_per_head = 2 * n_new

    def head_of(shape, row_axis, col_axis):
        r = lax.broadcasted_iota(jnp.int32, shape, row_axis)
        c = lax.broadcasted_iota(jnp.int32, shape, col_axis)
        return r, c, c // cols_per_head

    def update(blocks, causal):
        h_idx, c_idx, c_head = head_of((N_HEADS, n_cols), 0, 1)
        own_head = h_idx == c_head
        raw = [lax.dot_general(k2d, a_ref[...], (((1,), (1,)), ((), ())), preferred_element_type=F32)
               for k2d, _ in blocks]
        scores, valids = [], []
        for s in raw:
            n_tok = s.shape[0] // N_HEADS
            s3 = s.reshape(n_tok, N_HEADS, n_cols)
            valid = own_head[None]
            if causal:
                t_idx = lax.broadcasted_iota(jnp.int32, s3.shape, 0)
                q_idx = lax.broadcasted_iota(jnp.int32, s3.shape, 2) % n_new
                valid = valid & (t_idx <= q_idx)
            scores.append(s3)
            valids.append(valid)
        m_old = m_ref[...]
        m_new = m_old
        for s3, valid in zip(scores, valids):
            m_new = jnp.maximum(m_new, jnp.max(jnp.where(valid, s3, NEG), axis=0))
        alpha = jnp.exp2(m_old - m_new)
        l_new = alpha * l_ref[...]
        pv = None
        for (_, v2d), s3, valid in zip(blocks, scores, valids):
            p3 = jnp.where(valid, jnp.exp2(s3 - m_new[None]), 0.0)
            l_new = l_new + jnp.sum(p3, axis=0)
            part = lax.dot_general(v2d, p3.reshape(v2d.shape[0], n_cols), (((0,), (0,)), ((), ())),
                                   preferred_element_type=F32)
            pv = part if pv is None else pv + part
        l_ref[...] = l_new
        m_ref[...] = m_new
        alpha_row = jnp.sum(jnp.where(own_head, alpha, 0.0), axis=0, keepdims=True)
        acc_ref[...] = acc_ref[...] * alpha_row + pv

    @pl.when(p_idx == 0)
    def _():
        q = q_ref[0].astype(F32)
        lane = lax.broadcasted_iota(jnp.int32, (n_new, V_DIM), 1)
        rows = []
        for h in range(N_HEADS):
            qh = q[:, h * V_DIM:(h + 1) * V_DIM]
            rows += [jnp.where(lane < HEAD_DIM, qh, 0.0), jnp.where(lane >= HEAD_DIM, qh, 0.0)]
        a_ref[...] = jnp.concatenate(rows, axis=0)
        m_ref[...] = jnp.full(m_ref.shape, NEG, F32)
        l_ref[...] = jnp.zeros(l_ref.shape, F32)
        acc_ref[...] = jnp.zeros(acc_ref.shape, F32)
        update([(kn_ref[0], vn_ref[0])], True)

    rows = PAGE_SIZE * N_HEADS
    pages = [(ck[0].reshape(rows, V_DIM), cv[0].reshape(rows, V_DIM)) for ck, cv in zip(ck_refs, cv_refs)]
    update(pages, False)

    @pl.when(p_idx == pl.num_programs(1) - 1)
    def _():
        lam = _lambda_value(lam_ref, lam_init)
        h_idx, c_idx, c_head = head_of((N_HEADS, n_cols), 0, 1)
        l_row = jnp.sum(jnp.where(h_idx == c_head, l_ref[...], 0.0), axis=0, keepdims=True)
        on = (acc_ref[...] / l_row).T
        parts = []
        for h in range(N_HEADS):
            blk = on[h * cols_per_head:(h + 1) * cols_per_head]
            o = blk[:n_new] - lam * blk[n_new:]
            parts.append(_rms_rows(o, hn_ref[:, h * V_DIM:(h + 1) * V_DIM]) * (1.0 - lam_init))
        o_ref[0] = jnp.concatenate(parts, axis=-1).astype(BF16)


def _decode_attention(page_table, lam_vecs, q, k_new, v_new, cache_k, cache_v, head_norm, lam_init):
    b, n_new, width = q.shape
    n_pages = page_table.shape[1]
    n_cols = 2 * N_HEADS * n_new
    seq_spec = pl.BlockSpec((1, n_new, width), lambda bi, p, pt: (bi, 0, 0))
    new_spec = pl.BlockSpec((1, n_new * N_HEADS, V_DIM), lambda bi, p, pt: (bi, 0, 0))
    n_pg = DECODE_PAGES_PER_STEP
    page_specs = [pl.BlockSpec((1, PAGE_SIZE, N_HEADS, V_DIM),
                               lambda bi, p, pt, s=s: (pt[bi, p * n_pg + s], 0, 0, 0)) for s in range(n_pg)]
    grid_spec = pltpu.PrefetchScalarGridSpec(
        num_scalar_prefetch=1,
        grid=(b, n_pages // n_pg),
        in_specs=[pl.BlockSpec(lam_vecs.shape, lambda bi, p, pt: (0, 0)),
                  seq_spec, new_spec, new_spec, *page_specs, *page_specs,
                  pl.BlockSpec((1, width), lambda bi, p, pt: (0, 0))],
        out_specs=seq_spec,
        scratch_shapes=[pltpu.VMEM((n_cols, V_DIM), F32), pltpu.VMEM((N_HEADS, n_cols), F32),
                        pltpu.VMEM((N_HEADS, n_cols), F32), pltpu.VMEM((V_DIM, n_cols), F32)],
    )
    return pl.pallas_call(
        functools.partial(_decode_attn_kernel, n_new=n_new, n_pg=n_pg, lam_init=lam_init),
        grid_spec=grid_spec,
        out_shape=jax.ShapeDtypeStruct(q.shape, BF16),
        compiler_params=_params("arbitrary", "arbitrary"),
        name="decode_attention",
    )(page_table, lam_vecs, q, k_new, v_new, *([cache_k] * n_pg), *([cache_v] * n_pg),
      head_norm.reshape(1, width))


def _pool_branch(ext, n_seq, seq_len, pos, wp_ref, ps_ref):
    rows = POOL_HALO + seq_len

    def tokens(a):
        if n_seq == 1:
            return a[POOL_HALO:]
        return a.reshape(n_seq, rows, a.shape[-1])[:, POOL_HALO:, :].reshape(n_seq * seq_len, a.shape[-1])

    cur = ext
    outs = []
    for g, w in enumerate(POOL_WINDOWS):
        cur = cur + pltpu.roll(cur, w // 2, axis=0)
        c0 = g * POOL_GROUP
        win = tokens(cur[:, :POOL_GROUP])
        tok = tokens(ext[:, c0:c0 + POOL_GROUP])
        cnt = jnp.minimum(w, pos + 1).astype(F32)
        d = (win / cnt - tok).astype(BF16)
        outs.append(jnp.dot(d, wp_ref[g], preferred_element_type=F32) * ps_ref[:, c0:c0 + POOL_GROUP])
        cur = cur[:, POOL_GROUP:]
    return jnp.concatenate(outs, axis=-1)


def _merge_project(x, o_n, pool_o, sig_a, sig_p, wa_ref, wpb_ref, wo_ref):
    attn = jnp.dot(o_n, wa_ref[...], preferred_element_type=F32)
    pool = jnp.dot(pool_o.astype(BF16), wpb_ref[...], preferred_element_type=F32)
    merged = sig_a * attn + sig_p * pool
    return x + jnp.dot(merged.astype(BF16), wo_ref[...], preferred_element_type=F32)


def _prompt_mix_kernel(u_ref, halo_ref, on_ref, sa_ref, sp_ref, x_ref, wp_ref, ps_ref, wa_ref, wpb_ref, wo_ref,
                       x1_ref, *, tm):
    i = pl.program_id(1)
    halo = jnp.where(i > 0, halo_ref[0], 0.0)
    ext = jnp.concatenate([halo, u_ref[0]], axis=0)
    pos = i * tm + lax.broadcasted_iota(jnp.int32, (tm, 1), 0)
    pool_o = _pool_branch(ext, 1, tm, pos, wp_ref, ps_ref)
    x1_ref[0] = _merge_project(x_ref[0], on_ref[0], pool_o, sa_ref[0], sp_ref[0], wa_ref, wpb_ref, wo_ref)


def _prompt_mix(u, o_n, sig_a, sig_p, x, w_pool, pool_scale, w_attn, w_poolb, w_out, tm):
    b, t, d = x.shape
    tile = lambda bi, i: (bi, i, 0)
    fixed2 = lambda bi, i: (0, 0)
    halo_blocks = tm // POOL_HALO
    tile_spec = pl.BlockSpec((1, tm, d), tile)
    return pl.pallas_call(
        functools.partial(_prompt_mix_kernel, tm=tm),
        grid=(b, t // tm),
        in_specs=[tile_spec,
                  pl.BlockSpec((1, POOL_HALO, d), lambda bi, i: (bi, jnp.maximum(i * halo_blocks - 1, 0), 0)),
                  tile_spec, tile_spec, tile_spec, tile_spec,
                  pl.BlockSpec(w_pool.shape, lambda bi, i: (0, 0, 0)),
                  pl.BlockSpec((1, d), fixed2),
                  pl.BlockSpec(w_attn.shape, fixed2),
                  pl.BlockSpec(w_poolb.shape, fixed2),
                  pl.BlockSpec(w_out.shape, fixed2)],
        out_specs=tile_spec,
        out_shape=jax.ShapeDtypeStruct(x.shape, F32),
        compiler_params=_params("arbitrary", "arbitrary"),
        name="prompt_mix",
    )(u, u, o_n, sig_a, sig_p, x, w_pool, pool_scale, w_attn, w_poolb, w_out)


def _sample_mix_kernel(u_ref, hist_ref, on_ref, sa_ref, sp_ref, x_ref, wp_ref, ps_ref, wa_ref, wpb_ref, wo_ref,
                       x1_ref, state_ref, *, n_seq, seq_len, pos0):
    d = u_ref.shape[-1]
    rows = POOL_HALO + seq_len
    ext3 = jnp.concatenate([hist_ref[...], u_ref[...].reshape(n_seq, seq_len, d)], axis=1)
    state_ref[...] = ext3[:, rows - POOL_HALO:, :]
    pos = pos0 + lax.broadcasted_iota(jnp.int32, (n_seq * seq_len, 1), 0) % seq_len
    pool_o = _pool_branch(ext3.reshape(n_seq * rows, d), n_seq, seq_len, pos, wp_ref, ps_ref)
    x1_ref[...] = _merge_project(x_ref[...], on_ref[...], pool_o, sa_ref[...], sp_ref[...], wa_ref, wpb_ref, wo_ref)


def _sample_mix(u, hist, o_n, sig_a, sig_p, x, w_pool, pool_scale, w_attn, w_poolb, w_out, seq_len, pos0):
    n, d = x.shape
    n_seq = n // seq_len
    full = lambda a: pl.BlockSpec(a.shape, lambda i, nd=a.ndim: (0,) * nd)
    args = (u, hist, o_n, sig_a, sig_p, x, w_pool, pool_scale, w_attn, w_poolb, w_out)
    out_shape = [jax.ShapeDtypeStruct(x.shape, F32), jax.ShapeDtypeStruct(hist.shape, F32)]
    return pl.pallas_call(
        functools.partial(_sample_mix_kernel, n_seq=n_seq, seq_len=seq_len, pos0=pos0),
        grid=(1,),
        in_specs=[full(a) for a in args],
        out_specs=[full(s) for s in out_shape],
        out_shape=out_shape,
        compiler_params=_params("arbitrary"),
        name="sample_mix",
    )(*args)


def _conv_taps(up, hist, n_seq, seq_len):
    c = up.shape[-1]
    rows = CONV_HALO + seq_len
    if n_seq == 1:
        ext = jnp.concatenate([hist, up], axis=0)
    else:
        ext = jnp.concatenate([hist.reshape(n_seq, CONV_HALO, c), up.reshape(n_seq, seq_len, c)], axis=1)
        ext = ext.reshape(n_seq * rows, c)

    def tokens(a):
        if n_seq == 1:
            return a[CONV_HALO:]
        return a.reshape(n_seq, rows, c)[:, CONV_HALO:, :].reshape(n_seq * seq_len, c)

    return tokens(pltpu.roll(ext, 1, axis=0)), tokens(pltpu.roll(ext, 2, axis=0))


def _ffn_kernel(x1_ref, nf_ref, wg_ref, wv_ref, cwg_ref, cwv_ref, cbg_ref, cbv_ref, wd_ref, hg_ref, hv_ref,
                y_ref, sg_ref, sv_ref, h_ref, acc_ref, *, n_seq, seq_len):
    j = pl.program_id(1)
    n_j = pl.num_programs(1)

    @pl.when(j == 0)
    def _():
        h_ref[...] = _rms_rows(x1_ref[...], nf_ref[...]).astype(BF16)
        acc_ref[...] = jnp.zeros(acc_ref.shape, F32)

    h = h_ref[...]
    halves = []
    for w_ref, cw_ref, cb_ref, hist_ref, s_ref in ((wg_ref, cwg_ref, cbg_ref, hg_ref, sg_ref),
                                                   (wv_ref, cwv_ref, cbv_ref, hv_ref, sv_ref)):
        up = jnp.dot(h, w_ref[...], preferred_element_type=F32)
        s_ref[...] = up
        prev1, prev2 = _conv_taps(up, hist_ref[...], n_seq, seq_len)
        cw = cw_ref[...]
        halves.append(cb_ref[...] + prev2 * cw[0:1] + prev1 * cw[1:2] + up * cw[2:3])
    gate, val = halves
    act = (gate * _sigmoid(gate) * val).astype(BF16)
    acc_ref[...] += jnp.dot(act, wd_ref[...], preferred_element_type=F32)

    @pl.when(j == n_j - 1)
    def _():
        y_ref[...] = x1_ref[...] + acc_ref[...]


def _conv_ffn(x1, norm_ffn, w_up, conv_w, conv_b, w_down, hist_gate, hist_val, tm, seq_len):
    n, d = x1.shape
    d_ff = w_down.shape[0]
    n_chunks = d_ff // FF_CHUNK
    n_seq = tm // seq_len
    row = lambda i, j: (i, 0)
    gate_col = lambda i, j: (0, j)
    val_col = lambda i, j: (0, n_chunks + j)
    up_shape = jax.ShapeDtypeStruct((n, d_ff), F32)
    up_spec = pl.BlockSpec((tm, FF_CHUNK), lambda i, j: (i, j))
    hist_spec = pl.BlockSpec((n_seq * CONV_HALO, FF_CHUNK), lambda i, j: (i, j))
    return pl.pallas_call(
        functools.partial(_ffn_kernel, n_seq=n_seq, seq_len=seq_len),
        grid=(n // tm, n_chunks),
        in_specs=[pl.BlockSpec((tm, d), row),
                  pl.BlockSpec((1, d), lambda i, j: (0, 0)),
                  pl.BlockSpec((d, FF_CHUNK), gate_col),
                  pl.BlockSpec((d, FF_CHUNK), val_col),
                  pl.BlockSpec((CONV_W, FF_CHUNK), gate_col),
                  pl.BlockSpec((CONV_W, FF_CHUNK), val_col),
                  pl.BlockSpec((1, FF_CHUNK), gate_col),
                  pl.BlockSpec((1, FF_CHUNK), val_col),
                  pl.BlockSpec((FF_CHUNK, d), lambda i, j: (j, 0)),
                  hist_spec, hist_spec],
        out_specs=[pl.BlockSpec((tm, d), row), up_spec, up_spec],
        out_shape=[jax.ShapeDtypeStruct((n, d), F32), up_shape, up_shape],
        scratch_shapes=[pltpu.VMEM((tm, d), BF16), pltpu.VMEM((tm, d), F32)],
        compiler_params=_params("arbitrary", "arbitrary"),
        name="conv_ffn",
    )(x1, norm_ffn, w_up, w_up, conv_w, conv_w, conv_b, conv_b, w_down, hist_gate, hist_val)


def _ffn_seq_kernel(x1_ref, nf_ref, wu_ref, cw_ref, cb_ref, wd_ref, y_ref, state_ref,
                    h_ref, acc_ref, carry_ref, up0_ref, up1_ref):
    i = pl.program_id(1)
    tm = x1_ref.shape[0]
    d_ff = wd_ref.shape[0]
    n_chunks = d_ff // FF_CHUNK
    h_ref[...] = _rms_rows(x1_ref[...], nf_ref[...]).astype(BF16)
    acc_ref[...] = jnp.zeros(acc_ref.shape, F32)
    up_refs = (up0_ref, up1_ref)

    def cols(c, half):
        return slice(half * d_ff + c * FF_CHUNK, half * d_ff + (c + 1) * FF_CHUNK)

    def up_project(c):
        for half in range(2):
            up_refs[c % 2][half] = jnp.dot(h_ref[...], wu_ref[:, cols(c, half)], preferred_element_type=F32)

    def consume(c):
        part = tm // FFN_ROW_PARTS
        hists = [jnp.where(i > 0, carry_ref[:, cols(c, half)], 0.0) for half in range(2)]
        for r in range(FFN_ROW_PARTS):
            rows = slice(r * part, (r + 1) * part)
            halves = []
            for half in range(2):
                up = up_refs[c % 2][half, rows, :]
                prev1, prev2 = _conv_taps(up, hists[half], 1, part)
                hists[half] = up[part - CONV_HALO:]
                cw = cw_ref[:, cols(c, half)]
                halves.append(cb_ref[:, cols(c, half)] + prev2 * cw[0:1] + prev1 * cw[1:2] + up * cw[2:3])
            gate, val = halves
            act = (gate * _sigmoid(gate) * val).astype(BF16)
            acc_ref[rows, :] += jnp.dot(act, wd_ref[c * FF_CHUNK:(c + 1) * FF_CHUNK, :],
                                        preferred_element_type=F32)
        for half in range(2):
            carry_ref[:, cols(c, half)] = hists[half]
            state_ref[0, :, cols(c, half)] = hists[half]

    up_project(0)
    for c in range(n_chunks):
        if c + 1 < n_chunks:
            up_project(c + 1)
        consume(c)
    y_ref[...] = x1_ref[...] + acc_ref[...]


def _conv_ffn_seq(x1, norm_ffn, w_up, conv_w, conv_b, w_down, n_batch, tm):
    n, d = x1.shape
    d_ff = w_down.shape[0]
    tiles = n // (n_batch * tm)
    row = lambda bi, i: (bi * tiles + i, 0)
    whole = lambda a: pl.BlockSpec(a.shape, lambda bi, i, nd=a.ndim: (0,) * nd, pipeline_mode=pl.Buffered(1))
    state_shape = (n_batch * tiles, CONV_HALO, 2 * d_ff)
    return pl.pallas_call(
        _ffn_seq_kernel,
        grid=(n_batch, tiles),
        in_specs=[pl.BlockSpec((tm, d), row), whole(norm_ffn), whole(w_up), whole(conv_w), whole(conv_b),
                  whole(w_down)],
        out_specs=[pl.BlockSpec((tm, d), row),
                   pl.BlockSpec((1,) + state_shape[1:], lambda bi, i: (bi * tiles + i, 0, 0))],
        out_shape=[jax.ShapeDtypeStruct((n, d), F32), jax.ShapeDtypeStruct(state_shape, F32)],
        scratch_shapes=[pltpu.VMEM((tm, d), BF16), pltpu.VMEM((tm, d), F32),
                        pltpu.VMEM((CONV_HALO, 2 * d_ff), F32),
                        pltpu.VMEM((2, tm, FF_CHUNK), F32), pltpu.VMEM((2, tm, FF_CHUNK), F32)],
        compiler_params=_params("arbitrary", "arbitrary"),
        name="conv_ffn_seq",
    )(x1, norm_ffn, w_up, conv_w, conv_b, w_down)


PROMPT_TM = 512
FFN_TM = 512
DECODE_PAGES_PER_STEP = 8
ATTN_BLK_Q = 1024
ATTN_BLK_K = 512
ATTN_HEADS_PER_STEP = 2


def kernel(x_prompt, x_sample, cache_k, cache_v, state_pool, state_ffn_conv, page_table, norm_mix, w_in, q_norm, k_norm, lambda_q1, lambda_k1, lambda_q2, lambda_k2, head_norm, w_pool, pool_scale, w_attn_branch, w_pool_branch, w_out, norm_ffn, w_up, conv_w, conv_b, w_down):
    depth = w_in.shape[0]
    b, t, d = x_prompt.shape
    bs, ts, _ = x_sample.shape
    width = N_HEADS * V_DIM
    d_ff = w_down.shape[1]
    past_len = page_table.shape[1] * PAGE_SIZE

    yp, ys = x_prompt, x_sample.reshape(bs * ts, d)
    outs = [[] for _ in range(8)]
    for l in range(depth):
        lam_init = lambda_init_for(l)
        w_in_bf = w_in[l].astype(BF16)
        w_pool_bf = w_pool[l].astype(BF16)
        w_attn_bf = w_attn_branch[l].astype(BF16)
        w_poolb_bf = w_pool_branch[l].astype(BF16)
        w_out_bf = w_out[l].astype(BF16)
        w_up_bf = w_up[l].astype(BF16)
        w_down_bf = w_down[l].astype(BF16)
        nm = norm_mix[l].reshape(1, d)
        nf = norm_ffn[l].reshape(1, d)
        qg = q_norm[l].reshape(1, V_DIM)
        kg = k_norm[l].reshape(1, V_DIM)
        lam_vecs = jnp.stack([lambda_q1[l], lambda_k1[l], lambda_q2[l], lambda_k2[l]])
        ps = pool_scale[l].reshape(1, -1)
        cb = conv_b[l].reshape(1, -1)

        q, kf, kb, vf, vb, u, sa, sp = _in_projection(yp.reshape(b * t, d), nm, w_in_bf, qg, kg, PROMPT_TM)
        r3 = lambda a: a.reshape(b, t, -1)
        o_n = _prompt_attention(lam_vecs, r3(q), r3(kb), vb, head_norm[l], lam_init,
                                ATTN_BLK_Q, ATTN_BLK_K, ATTN_HEADS_PER_STEP)
        x1 = _prompt_mix(r3(u), o_n, r3(sa), r3(sp), yp, w_pool_bf, ps, w_attn_bf, w_poolb_bf, w_out_bf, PROMPT_TM)
        y2d, f_tails = _conv_ffn_seq(x1.reshape(b * t, d), nf, w_up_bf, conv_w[l], cb, w_down_bf, b, FFN_TM)
        yp = y2d.reshape(b, t, d)
        outs[0].append(kf.reshape(b, t, N_HEADS, V_DIM))
        outs[1].append(vf.reshape(b, t, N_HEADS, V_DIM))
        outs[2].append(r3(u)[:, t - POOL_BUF:, :])
        last = f_tails.reshape((b, -1) + f_tails.shape[1:])[:, -1]
        outs[3].append(last[:, CONV_HALO - (CONV_W - 1):, :])

        n_s = bs * ts
        q, kf, kb, vf, vb, u, sa, sp = _in_projection(ys, nm, w_in_bf, qg, kg, n_s)
        s3 = lambda a: a.reshape(bs, ts, -1)
        o_n = _decode_attention(page_table, lam_vecs, s3(q),
                                kf.reshape(bs, ts * N_HEADS, V_DIM), vf.reshape(bs, ts * N_HEADS, V_DIM),
                                cache_k[l], cache_v[l], head_norm[l], lam_init)
        pool_hist = jnp.pad(state_pool[l], ((0, 0), (POOL_HALO - POOL_BUF, 0), (0, 0)))
        x1, pool_state = _sample_mix(u, pool_hist, o_n.reshape(n_s, width), sa, sp, ys, w_pool_bf, ps,
                                     w_attn_bf, w_poolb_bf, w_out_bf, ts, past_len)
        conv_hist = jnp.pad(state_ffn_conv[l], ((0, 0), (CONV_HALO - (CONV_W - 1), 0), (0, 0)))
        conv_hist = conv_hist.reshape(bs * CONV_HALO, 2 * d_ff)
        ys, fg, fv = _conv_ffn(x1, nf, w_up_bf, conv_w[l], cb, w_down_bf,
                               conv_hist[:, :d_ff], conv_hist[:, d_ff:], tm=n_s, seq_len=ts)
        outs[4].append(kf.reshape(bs, ts, N_HEADS, V_DIM))
        outs[5].append(vf.reshape(bs, ts, N_HEADS, V_DIM))
        outs[6].append(pool_state[:, POOL_HALO - POOL_BUF:, :])
        ffn_state = jnp.concatenate([fg, fv], axis=-1).reshape(bs, ts, 2 * d_ff)
        outs[7].append(ffn_state[:, ts - (CONV_W - 1):, :])

    return (yp, ys.reshape(bs, ts, d)) + tuple(jnp.stack(o) for o in outs)
```

```python
import functools
import math

import jax
import jax.numpy as jnp
from jax import lax
from jax.experimental import pallas as pl
from jax.experimental.pallas import tpu as pltpu

F32 = jnp.float32
BF16 = jnp.bfloat16

N_HEADS = 8
HEAD_DIM = 64
V_DIM = 2 * HEAD_DIM
POOL_WINDOWS = (2, 4, 8, 16)
POOL_GROUP = 256
POOL_BUF = max(POOL_WINDOWS) - 1
POOL_HALO = 16
CONV_W = 3
CONV_HALO = 8
PAGE_SIZE = 128
SUM_ROWS = 16
EPS = 1e-6
NEG = -1e30
Q_SCALE = HEAD_DIM ** -0.5 * math.log2(math.e)
FF_CHUNK = 256
FFN_SEQ_CHUNK = 1024
FFN_ROW_PARTS = 2
IN_SECTION = 1024
IN_PART_ROWS = 256
VMEM_LIMIT = 56 * 1024 * 1024


def lambda_init_for(layer):
    return 0.8 - 0.6 * math.exp(-0.3 * layer)


def _params(*sem):
    return pltpu.CompilerParams(dimension_semantics=sem, vmem_limit_bytes=VMEM_LIMIT)


def _sigmoid(x):
    return 1.0 / (1.0 + jnp.exp(-x))


def _rms_rows(x, g):
    return x * lax.rsqrt(jnp.mean(x * x, axis=-1, keepdims=True) + EPS) * g


def _lambda_value(lam_ref, lam_init):
    lv = lam_ref[...]
    s1 = jnp.sum(lv[0:1] * lv[1:2], axis=-1, keepdims=True)
    s2 = jnp.sum(lv[2:3] * lv[3:4], axis=-1, keepdims=True)
    return jnp.exp(s1) - jnp.exp(s2) + lam_init


def _inproj_kernel(x_ref, nm_ref, w_ref, qg_ref, kg_ref,
                   q_ref, kf_ref, kb_ref, vf_ref, vb_ref, u_ref, sa_ref, sp_ref, h_ref, *, n_split):
    part = x_ref.shape[0] // n_split
    h_ref[...] = _rms_rows(x_ref[...], nm_ref[...]).astype(BF16)

    def section(j, epilogue):
        w = w_ref[:, j * IN_SECTION:(j + 1) * IN_SECTION]
        for r in range(n_split):
            rows = slice(r * part, (r + 1) * part)
            epilogue(rows, jnp.dot(h_ref[rows, :], w, preferred_element_type=F32))

    def group_norm(y, g_ref):
        r = lax.broadcasted_iota(jnp.int32, (V_DIM, V_DIM), 0) // HEAD_DIM
        c = lax.broadcasted_iota(jnp.int32, (V_DIM, V_DIM), 1) // HEAD_DIM
        ones_bd = (r == c).astype(BF16)
        parts = []
        for h in range(N_HEADS):
            yh = y[:, h * V_DIM:(h + 1) * V_DIM]
            ss = jnp.dot((yh * yh).astype(BF16), ones_bd, preferred_element_type=F32)
            parts.append(yh * lax.rsqrt(ss * (1.0 / HEAD_DIM) + EPS) * g_ref[...])
        return jnp.concatenate(parts, axis=-1)

    def store_heads(dst_ref, rows, val):
        for h in range(N_HEADS):
            dst_ref[rows, h, :] = val[:, h * V_DIM:(h + 1) * V_DIM]

    def q_epilogue(rows, y):
        q_ref[rows, :] = (group_norm(y, qg_ref) * Q_SCALE).astype(BF16)

    def k_epilogue(rows, y):
        kn = group_norm(y, kg_ref)
        store_heads(kf_ref, rows, kn)
        kb_ref[rows, :] = kn.astype(BF16)

    def v_epilogue(rows, y):
        store_heads(vf_ref, rows, y)
        vb_ref[:, rows] = y.T.astype(BF16)

    def u_epilogue(rows, y):
        u_ref[rows, :] = y

    def gate_epilogue(dst_ref):
        def epilogue(rows, y):
            dst_ref[rows, :] = _sigmoid(y).astype(BF16)
        return epilogue

    for j, epilogue in enumerate((q_epilogue, k_epilogue, v_epilogue, u_epilogue,
                                  gate_epilogue(sa_ref), gate_epilogue(sp_ref))):
        section(j, epilogue)


def _in_projection(x2d, norm_mix, w_in_bf, q_gain, k_gain, tm):
    n, d = x2d.shape
    row = lambda i: (i, 0)
    fixed = lambda i: (0, 0)
    tile = pl.BlockSpec((tm, IN_SECTION), row)
    heads_tile = pl.BlockSpec((tm, N_HEADS, V_DIM), lambda i: (i, 0, 0))
    flat = lambda dt: jax.ShapeDtypeStruct((n, IN_SECTION), dt)
    heads = jax.ShapeDtypeStruct((n, N_HEADS, V_DIM), F32)
    v_t_tile = pl.BlockSpec((IN_SECTION, tm), lambda i: (0, i))
    v_t = jax.ShapeDtypeStruct((IN_SECTION, n), BF16)
    out_specs = [tile, heads_tile, tile, heads_tile, v_t_tile, tile, tile, tile]
    out_shape = [flat(BF16), heads, flat(BF16), heads, v_t, flat(F32), flat(BF16), flat(BF16)]
    return pl.pallas_call(
        functools.partial(_inproj_kernel, n_split=max(1, tm // IN_PART_ROWS)),
        grid=(n // tm,),
        in_specs=[pl.BlockSpec((tm, d), row),
                  pl.BlockSpec((1, d), fixed),
                  pl.BlockSpec(w_in_bf.shape, fixed, pipeline_mode=pl.Buffered(1)),
                  pl.BlockSpec((1, V_DIM), fixed),
                  pl.BlockSpec((1, V_DIM), fixed)],
        out_specs=out_specs,
        out_shape=out_shape,
        scratch_shapes=[pltpu.VMEM((tm, d), BF16)],
        compiler_params=_params("arbitrary"),
        name="in_projection",
    )(x2d, norm_mix, w_in_bf, q_gain, k_gain)


def _prompt_attn_kernel(lam_ref, q_ref, k_ref, vt_ref, hn_ref, o_ref, m_ref, acc_ref, st0_ref, st1_ref,
                        *, blk_q, blk_k, n_hd, lam_init):
    i = pl.program_id(2)
    lane = lax.broadcasted_iota(jnp.int32, (blk_q, V_DIM), 1)
    q_comp = []
    for hd in range(n_hd):
        q = q_ref[0, :, hd * V_DIM:(hd + 1) * V_DIM]
        zero = jnp.zeros_like(q)
        q_comp.append((jnp.where(lane < HEAD_DIM, q, zero), jnp.where(lane >= HEAD_DIM, q, zero)))

    m_ref[...] = jnp.full(m_ref.shape, NEG, F32)
    acc_ref[...] = jnp.zeros(acc_ref.shape, F32)
    ones_rows = jnp.ones((SUM_ROWS, blk_k), BF16)

    def scores(j, st_ref, q_lo=0):
        start = pl.multiple_of(j * blk_k, blk_k)
        for hd in range(n_hd):
            k = k_ref[0, pl.ds(start, blk_k), hd * V_DIM:(hd + 1) * V_DIM]
            for c in range(2):
                st_ref[hd, c, :, q_lo:] = lax.dot_general(k, q_comp[hd][c][q_lo:], (((1,), (1,)), ((), ())),
                                                          preferred_element_type=F32)

    def update(j, st_ref, masked, q_lo=0):
        start = pl.multiple_of(j * blk_k, blk_k)
        for hd in range(n_hd):
            vt = jnp.concatenate([vt_ref[hd * V_DIM:(hd + 1) * V_DIM, pl.ds(start, blk_k)], ones_rows], axis=0)
            for c in range(2):
                st = st_ref[hd, c, :, q_lo:]
                if masked:
                    k_idx = lax.broadcasted_iota(jnp.int32, (blk_k, blk_k), 0)
                    q_idx = lax.broadcasted_iota(jnp.int32, (blk_k, blk_k), 1)
                    causal = jnp.where(k_idx <= q_idx, st[:, :blk_k], NEG)
                    st = causal if st.shape[1] == blk_k else jnp.concatenate([causal, st[:, blk_k:]], axis=1)
                m_old = m_ref[hd, c, :, q_lo:]
                m_new = jnp.maximum(m_old, jnp.max(st, axis=0, keepdims=True))
                pt = jnp.exp2(st - m_new)
                alpha = jnp.exp2(m_old - m_new)
                acc_ref[hd, c, :, q_lo:] = (alpha * acc_ref[hd, c, :, q_lo:]
                                            + jnp.dot(vt, pt.astype(BF16), preferred_element_type=F32))
                m_ref[hd, c, :, q_lo:] = m_new

    def body(jj, carry):
        scores(2 * jj + 1, st1_ref)
        update(2 * jj, st0_ref, False)
        scores(2 * jj + 2, st0_ref)
        update(2 * jj + 1, st1_ref, False)
        return carry

    assert blk_q == 2 * blk_k
    scores(0, st0_ref)
    lax.fori_loop(0, i, body, 0)
    scores(2 * i + 1, st1_ref, q_lo=blk_k)
    update(2 * i, st0_ref, True)
    update(2 * i + 1, st1_ref, True, q_lo=blk_k)

    lam = _lambda_value(lam_ref, lam_init)
    norm = lambda a: a[:V_DIM] / a[V_DIM:V_DIM + 1]
    for hd in range(n_hd):
        ot = norm(acc_ref[hd, 0]) - lam * norm(acc_ref[hd, 1])
        gain = hn_ref[hd * V_DIM:(hd + 1) * V_DIM, :]
        ot = ot * lax.rsqrt(jnp.mean(ot * ot, axis=0, keepdims=True) + EPS) * gain * (1.0 - lam_init)
        o_ref[0, :, hd * V_DIM:(hd + 1) * V_DIM] = ot.T.astype(BF16)


def _prompt_attention(lam_vecs, q, k, v_t, head_norm, lam_init, blk_q, blk_k, n_hd):
    b, t, _ = q.shape
    q_spec = pl.BlockSpec((1, blk_q, n_hd * V_DIM), lambda bi, h, i: (bi, i, h))
    return pl.pallas_call(
        functools.partial(_prompt_attn_kernel, blk_q=blk_q, blk_k=blk_k, n_hd=n_hd, lam_init=lam_init),
        grid=(b, N_HEADS // n_hd, t // blk_q),
        in_specs=[pl.BlockSpec(lam_vecs.shape, lambda bi, h, i: (0, 0)),
                  q_spec,
                  pl.BlockSpec((1, t, n_hd * V_DIM), lambda bi, h, i: (bi, 0, h)),
                  pl.BlockSpec((n_hd * V_DIM, t), lambda bi, h, i: (h, bi)),
                  pl.BlockSpec((n_hd * V_DIM, 1), lambda bi, h, i: (h, 0))],
        out_specs=q_spec,
        out_shape=jax.ShapeDtypeStruct(q.shape, BF16),
        scratch_shapes=[pltpu.VMEM((n_hd, 2, 1, blk_q), F32),
                        pltpu.VMEM((n_hd, 2, V_DIM + SUM_ROWS, blk_q), F32),
                        pltpu.VMEM((n_hd, 2, blk_k, blk_q), F32), pltpu.VMEM((n_hd, 2, blk_k, blk_q), F32)],
        compiler_params=_params("arbitrary", "arbitrary", "arbitrary"),
        name="prompt_attention",
    )(lam_vecs, q, k, v_t, head_norm.reshape(N_HEADS * V_DIM, 1))


def _decode_attn_kernel(pt_ref, lam_ref, q_ref, kn_ref, vn_ref, *refs, n_new, n_pg, lam_init):
    del pt_ref
    ck_refs, cv_refs = refs[:n_pg], refs[n_pg:2 * n_pg]
    hn_ref, o_ref, a_ref, m_ref, l_ref, acc_ref = refs[2 * n_pg:]
    p_idx = pl.program_id(1)
    n_cols = 2 * N_HEADS * n_new
    cols_per_head = 2 * n_new

    def head_of(shape, row_axis, col_axis):
        r = lax.broadcasted_iota(jnp.int32, shape, row_axis)
        c = lax.broadcasted_iota(jnp.int32, shape, col_axis)
        return r, c, c // cols_per_head

    def update(blocks, causal):
        h_idx, c_idx, c_head = head_of((N_HEADS, n_cols), 0, 1)
        own_head = h_idx == c_head
        raw = [lax.dot_general(k2d, a_ref[...], (((1,), (1,)), ((), ())), preferred_element_type=F32)
               for k2d, _ in blocks]
        scores, valids = [], []
        for s in raw:
            n_tok = s.shape[0] // N_HEADS
            s3 = s.reshape(n_tok, N_HEADS, n_cols)
            valid = own_head[None]
            if causal:
                t_idx = lax.broadcasted_iota(jnp.int32, s3.shape, 0)
                q_idx = lax.broadcasted_iota(jnp.int32, s3.shape, 2) % n_new
                valid = valid & (t_idx <= q_idx)
            scores.append(s3)
            valids.append(valid)
        m_old = m_ref[...]
        m_new = m_old
        for s3, valid in zip(scores, valids):
            m_new = jnp.maximum(m_new, jnp.max(jnp.where(valid, s3, NEG), axis=0))
        alpha = jnp.exp2(m_old - m_new)
        l_new = alpha * l_ref[...]
        pv = None
        for (_, v2d), s3, valid in zip(blocks, scores, valids):
            p3 = jnp.where(valid, jnp.exp2(s3 - m_new[None]), 0.0)
            l_new = l_new + jnp.sum(p3, axis=0)
            part = lax.dot_general(v2d, p3.reshape(v2d.shape[0], n_cols), (((0,), (0,)), ((), ())),
                                   preferred_element_type=F32)
            pv = part if pv is None else pv + part
        l_ref[...] = l_new
        m_ref[...] = m_new
        alpha_row = jnp.sum(jnp.where(own_head, alpha, 0.0), axis=0, keepdims=True)
        acc_ref[...] = acc_ref[...] * alpha_row + pv

    @pl.when(p_idx == 0)
    def _():
        q = q_ref[0].astype(F32)
        lane = lax.broadcasted_iota(jnp.int32, (n_new, V_DIM), 1)
        rows = []
        for h in range(N_HEADS):
            qh = q[:, h * V_DIM:(h + 1) * V_DIM]
            rows += [jnp.where(lane < HEAD_DIM, qh, 0.0), jnp.where(lane >= HEAD_DIM, qh, 0.0)]
        a_ref[...] = jnp.concatenate(rows, axis=0)
        m_ref[...] = jnp.full(m_ref.shape, NEG, F32)
        l_ref[...] = jnp.zeros(l_ref.shape, F32)
        acc_ref[...] = jnp.zeros(acc_ref.shape, F32)
        update([(kn_ref[0], vn_ref[0])], True)

    rows = PAGE_SIZE * N_HEADS
    pages = [(ck[0].reshape(rows, V_DIM), cv[0].reshape(rows, V_DIM)) for ck, cv in zip(ck_refs, cv_refs)]
    update(pages, False)

    @pl.when(p_idx == pl.num_programs(1) - 1)
    def _():
        lam = _lambda_value(lam_ref, lam_init)
        h_idx, c_idx, c_head = head_of((N_HEADS, n_cols), 0, 1)
        l_row = jnp.sum(jnp.where(h_idx == c_head, l_ref[...], 0.0), axis=0, keepdims=True)
        on = (acc_ref[...] / l_row).T
        parts = []
        for h in range(N_HEADS):
            blk = on[h * cols_per_head:(h + 1) * cols_per_head]
            o = blk[:n_new] - lam * blk[n_new:]
            parts.append(_rms_rows(o, hn_ref[:, h * V_DIM:(h + 1) * V_DIM]) * (1.0 - lam_init))
        o_ref[0] = jnp.concatenate(parts, axis=-1).astype(BF16)


def _decode_attention(page_table, lam_vecs, q, k_new, v_new, cache_k, cache_v, head_norm, lam_init):
    b, n_new, width = q.shape
    n_pages = page_table.shape[1]
    n_cols = 2 * N_HEADS * n_new
    seq_spec = pl.BlockSpec((1, n_new, width), lambda bi, p, pt: (bi, 0, 0))
    new_spec = pl.BlockSpec((1, n_new * N_HEADS, V_DIM), lambda bi, p, pt: (bi, 0, 0))
    n_pg = DECODE_PAGES_PER_STEP
    page_specs = [pl.BlockSpec((1, PAGE_SIZE, N_HEADS, V_DIM),
                               lambda bi, p, pt, s=s: (pt[bi, p * n_pg + s], 0, 0, 0)) for s in range(n_pg)]
    grid_spec = pltpu.PrefetchScalarGridSpec(
        num_scalar_prefetch=1,
        grid=(b, n_pages // n_pg),
        in_specs=[pl.BlockSpec(lam_vecs.shape, lambda bi, p, pt: (0, 0)),
                  seq_spec, new_spec, new_spec, *page_specs, *page_specs,
                  pl.BlockSpec((1, width), lambda bi, p, pt: (0, 0))],
        out_specs=seq_spec,
        scratch_shapes=[pltpu.VMEM((n_cols, V_DIM), F32), pltpu.VMEM((N_HEADS, n_cols), F32),
                        pltpu.VMEM((N_HEADS, n_cols), F32), pltpu.VMEM((V_DIM, n_cols), F32)],
    )
    return pl.pallas_call(
        functools.partial(_decode_attn_kernel, n_new=n_new, n_pg=n_pg, lam_init=lam_init),
        grid_spec=grid_spec,
        out_shape=jax.ShapeDtypeStruct(q.shape, BF16),
        compiler_params=_params("arbitrary", "arbitrary"),
        name="decode_attention",
    )(page_table, lam_vecs, q, k_new, v_new, *([cache_k] * n_pg), *([cache_v] * n_pg),
      head_norm.reshape(1, width))


def _pool_branch(ext, n_seq, seq_len, pos, wp_ref, ps_ref):
    rows = POOL_HALO + seq_len

    def tokens(a):
        if n_seq == 1:
            return a[POOL_HALO:]
        return a.reshape(n_seq, rows, a.shape[-1])[:, POOL_HALO:, :].reshape(n_seq * seq_len, a.shape[-1])

    cur = ext
    outs = []
    for g, w in enumerate(POOL_WINDOWS):
        cur = cur + pltpu.roll(cur, w // 2, axis=0)
        c0 = g * POOL_GROUP
        win = tokens(cur[:, :POOL_GROUP])
        tok = tokens(ext[:, c0:c0 + POOL_GROUP])
        cnt = jnp.minimum(w, pos + 1).astype(F32)
        d = (win / cnt - tok).astype(BF16)
        outs.append(jnp.dot(d, wp_ref[g], preferred_element_type=F32) * ps_ref[:, c0:c0 + POOL_GROUP])
        cur = cur[:, POOL_GROUP:]
    return jnp.concatenate(outs, axis=-1)


def _merge_project(x, o_n, pool_o, sig_a, sig_p, wa_ref, wpb_ref, wo_ref):
    attn = jnp.dot(o_n, wa_ref[...], preferred_element_type=F32)
    pool = jnp.dot(pool_o.astype(BF16), wpb_ref[...], preferred_element_type=F32)
    merged = sig_a * attn + sig_p * pool
    return x + jnp.dot(merged.astype(BF16), wo_ref[...], preferred_element_type=F32)


def _prompt_mix_kernel(u_ref, halo_ref, on_ref, sa_ref, sp_ref, x_ref, wp_ref, ps_ref, wa_ref, wpb_ref, wo_ref,
                       x1_ref, *, tm):
    i = pl.program_id(1)
    halo = jnp.where(i > 0, halo_ref[0], 0.0)
    ext = jnp.concatenate([halo, u_ref[0]], axis=0)
    pos = i * tm + lax.broadcasted_iota(jnp.int32, (tm, 1), 0)
    pool_o = _pool_branch(ext, 1, tm, pos, wp_ref, ps_ref)
    x1_ref[0] = _merge_project(x_ref[0], on_ref[0], pool_o, sa_ref[0], sp_ref[0], wa_ref, wpb_ref, wo_ref)


def _prompt_mix(u, o_n, sig_a, sig_p, x, w_pool, pool_scale, w_attn, w_poolb, w_out, tm):
    b, t, d = x.shape
    tile = lambda bi, i: (bi, i, 0)
    fixed2 = lambda bi, i: (0, 0)
    halo_blocks = tm // POOL_HALO
    tile_spec = pl.BlockSpec((1, tm, d), tile)
    return pl.pallas_call(
        functools.partial(_prompt_mix_kernel, tm=tm),
        grid=(b, t // tm),
        in_specs=[tile_spec,
                  pl.BlockSpec((1, POOL_HALO, d), lambda bi, i: (bi, jnp.maximum(i * halo_blocks - 1, 0), 0)),
                  tile_spec, tile_spec, tile_spec, tile_spec,
                  pl.BlockSpec(w_pool.shape, lambda bi, i: (0, 0, 0)),
                  pl.BlockSpec((1, d), fixed2),
                  pl.BlockSpec(w_attn.shape, fixed2),
                  pl.BlockSpec(w_poolb.shape, fixed2),
                  pl.BlockSpec(w_out.shape, fixed2)],
        out_specs=tile_spec,
        out_shape=jax.ShapeDtypeStruct(x.shape, F32),
        compiler_params=_params("arbitrary", "arbitrary"),
        name="prompt_mix",
    )(u, u, o_n, sig_a, sig_p, x, w_pool, pool_scale, w_attn, w_poolb, w_out)


def _sample_mix_kernel(u_ref, hist_ref, on_ref, sa_ref, sp_ref, x_ref, wp_ref, ps_ref, wa_ref, wpb_ref, wo_ref,
                       x1_ref, state_ref, *, n_seq, seq_len, pos0):
    d = u_ref.shape[-1]
    rows = POOL_HALO + seq_len
    ext3 = jnp.concatenate([hist_ref[...], u_ref[...].reshape(n_seq, seq_len, d)], axis=1)
    state_ref[...] = ext3[:, rows - POOL_HALO:, :]
    pos = pos0 + lax.broadcasted_iota(jnp.int32, (n_seq * seq_len, 1), 0) % seq_len
    pool_o = _pool_branch(ext3.reshape(n_seq * rows, d), n_seq, seq_len, pos, wp_ref, ps_ref)
    x1_ref[...] = _merge_project(x_ref[...], on_ref[...], pool_o, sa_ref[...], sp_ref[...], wa_ref, wpb_ref, wo_ref)


def _sample_mix(u, hist, o_n, sig_a, sig_p, x, w_pool, pool_scale, w_attn, w_poolb, w_out, seq_len, pos0):
    n, d = x.shape
    n_seq = n // seq_len
    full = lambda a: pl.BlockSpec(a.shape, lambda i, nd=a.ndim: (0,) * nd)
    args = (u, hist, o_n, sig_a, sig_p, x, w_pool, pool_scale, w_attn, w_poolb, w_out)
    out_shape = [jax.ShapeDtypeStruct(x.shape, F32), jax.ShapeDtypeStruct(hist.shape, F32)]
    return pl.pallas_call(
        functools.partial(_sample_mix_kernel, n_seq=n_seq, seq_len=seq_len, pos0=pos0),
        grid=(1,),
        in_specs=[full(a) for a in args],
        out_specs=[full(s) for s in out_shape],
        out_shape=out_shape,
        compiler_params=_params("arbitrary"),
        name="sample_mix",
    )(*args)


def _conv_taps(up, hist, n_seq, seq_len):
    c = up.shape[-1]
    rows = CONV_HALO + seq_len
    if n_seq == 1:
        ext = jnp.concatenate([hist, up], axis=0)
    else:
        ext = jnp.concatenate([hist.reshape(n_seq, CONV_HALO, c), up.reshape(n_seq, seq_len, c)], axis=1)
        ext = ext.reshape(n_seq * rows, c)

    def tokens(a):
        if n_seq == 1:
            return a[CONV_HALO:]
        return a.reshape(n_seq, rows, c)[:, CONV_HALO:, :].reshape(n_seq * seq_len, c)

    return tokens(pltpu.roll(ext, 1, axis=0)), tokens(pltpu.roll(ext, 2, axis=0))


def _ffn_kernel(x1_ref, nf_ref, wg_ref, wv_ref, cwg_ref, cwv_ref, cbg_ref, cbv_ref, wd_ref, hg_ref, hv_ref,
                y_ref, sg_ref, sv_ref, h_ref, acc_ref, *, n_seq, seq_len):
    j = pl.program_id(1)
    n_j = pl.num_programs(1)

    @pl.when(j == 0)
    def _():
        h_ref[...] = _rms_rows(x1_ref[...], nf_ref[...]).astype(BF16)
        acc_ref[...] = jnp.zeros(acc_ref.shape, F32)

    h = h_ref[...]
    halves = []
    for w_ref, cw_ref, cb_ref, hist_ref, s_ref in ((wg_ref, cwg_ref, cbg_ref, hg_ref, sg_ref),
                                                   (wv_ref, cwv_ref, cbv_ref, hv_ref, sv_ref)):
        up = jnp.dot(h, w_ref[...], preferred_element_type=F32)
        s_ref[...] = up
        prev1, prev2 = _conv_taps(up, hist_ref[...], n_seq, seq_len)
        cw = cw_ref[...]
        halves.append(cb_ref[...] + prev2 * cw[0:1] + prev1 * cw[1:2] + up * cw[2:3])
    gate, val = halves
    act = (gate * _sigmoid(gate) * val).astype(BF16)
    acc_ref[...] += jnp.dot(act, wd_ref[...], preferred_element_type=F32)

    @pl.when(j == n_j - 1)
    def _():
        y_ref[...] = x1_ref[...] + acc_ref[...]


def _conv_ffn(x1, norm_ffn, w_up, conv_w, conv_b, w_down, hist_gate, hist_val, tm, seq_len):
    n, d = x1.shape
    d_ff = w_down.shape[0]
    n_chunks = d_ff // FF_CHUNK
    n_seq = tm // seq_len
    row = lambda i, j: (i, 0)
    gate_col = lambda i, j: (0, j)
    val_col = lambda i, j: (0, n_chunks + j)
    up_shape = jax.ShapeDtypeStruct((n, d_ff), F32)
    up_spec = pl.BlockSpec((tm, FF_CHUNK), lambda i, j: (i, j))
    hist_spec = pl.BlockSpec((n_seq * CONV_HALO, FF_CHUNK), lambda i, j: (i, j))
    return pl.pallas_call(
        functools.partial(_ffn_kernel, n_seq=n_seq, seq_len=seq_len),
        grid=(n // tm, n_chunks),
        in_specs=[pl.BlockSpec((tm, d), row),
                  pl.BlockSpec((1, d), lambda i, j: (0, 0)),
                  pl.BlockSpec((d, FF_CHUNK), gate_col),
                  pl.BlockSpec((d, FF_CHUNK), val_col),
                  pl.BlockSpec((CONV_W, FF_CHUNK), gate_col),
                  pl.BlockSpec((CONV_W, FF_CHUNK), val_col),
                  pl.BlockSpec((1, FF_CHUNK), gate_col),
                  pl.BlockSpec((1, FF_CHUNK), val_col),
                  pl.BlockSpec((FF_CHUNK, d), lambda i, j: (j, 0)),
                  hist_spec, hist_spec],
        out_specs=[pl.BlockSpec((tm, d), row), up_spec, up_spec],
        out_shape=[jax.ShapeDtypeStruct((n, d), F32), up_shape, up_shape],
        scratch_shapes=[pltpu.VMEM((tm, d), BF16), pltpu.VMEM((tm, d), F32)],
        compiler_params=_params("arbitrary", "arbitrary"),
        name="conv_ffn",
    )(x1, norm_ffn, w_up, w_up, conv_w, conv_w, conv_b, conv_b, w_down, hist_gate, hist_val)


def _ffn_seq_kernel(x1_ref, nf_ref, wu_ref, cw_ref, cb_ref, wd_ref, y_ref, state_ref,
                    h_ref, acc_ref, carry_ref, up0_ref, up1_ref):
    i = pl.program_id(1)
    tm = x1_ref.shape[0]
    d_ff = wd_ref.shape[0]
    bounds = list(range(0, d_ff, FFN_SEQ_CHUNK)) + [d_ff]
    n_chunks = len(bounds) - 1
    h_ref[...] = _rms_rows(x1_ref[...], nf_ref[...]).astype(BF16)
    acc_ref[...] = jnp.zeros(acc_ref.shape, F32)
    up_refs = (up0_ref, up1_ref)

    def cols(c, half):
        return slice(half * d_ff + bounds[c], half * d_ff + bounds[c + 1])

    def up_project(c):
        width = bounds[c + 1] - bounds[c]
        for half in range(2):
            up_refs[c % 2][half, :, :width] = jnp.dot(h_ref[...], wu_ref[:, cols(c, half)],
                                                      preferred_element_type=F32)

    def consume(c):
        part = tm // FFN_ROW_PARTS
        width = bounds[c + 1] - bounds[c]
        hists = [jnp.where(i > 0, carry_ref[:, cols(c, half)], 0.0) for half in range(2)]
        for r in range(FFN_ROW_PARTS):
            rows = slice(r * part, (r + 1) * part)
            halves = []
            for half in range(2):
                up = up_refs[c % 2][half, rows, :width]
                prev1, prev2 = _conv_taps(up, hists[half], 1, part)
                hists[half] = up[part - CONV_HALO:]
                cw = cw_ref[:, cols(c, half)]
                halves.append(cb_ref[:, cols(c, half)] + prev2 * cw[0:1] + prev1 * cw[1:2] + up * cw[2:3])
            gate, val = halves
            act = (gate * _sigmoid(gate) * val).astype(BF16)
            acc_ref[rows, :] += jnp.dot(act, wd_ref[bounds[c]:bounds[c + 1], :], preferred_element_type=F32)
        for half in range(2):
            carry_ref[:, cols(c, half)] = hists[half]
            state_ref[0, :, cols(c, half)] = hists[half]

    up_project(0)
    for c in range(n_chunks):
        if c + 1 < n_chunks:
            up_project(c + 1)
        consume(c)
    y_ref[...] = x1_ref[...] + acc_ref[...]


def _conv_ffn_seq(x1, norm_ffn, w_up, conv_w, conv_b, w_down, n_batch, tm):
    n, d = x1.shape
    d_ff = w_down.shape[0]
    tiles = n // (n_batch * tm)
    row = lambda bi, i: (bi * tiles + i, 0)
    whole = lambda a: pl.BlockSpec(a.shape, lambda bi, i, nd=a.ndim: (0,) * nd, pipeline_mode=pl.Buffered(1))
    state_shape = (n_batch * tiles, CONV_HALO, 2 * d_ff)
    return pl.pallas_call(
        _ffn_seq_kernel,
        grid=(n_batch, tiles),
        in_specs=[pl.BlockSpec((tm, d), row), whole(norm_ffn), whole(w_up), whole(conv_w), whole(conv_b),
                  whole(w_down)],
        out_specs=[pl.BlockSpec((tm, d), row),
                   pl.BlockSpec((1,) + state_shape[1:], lambda bi, i: (bi * tiles + i, 0, 0))],
        out_shape=[jax.ShapeDtypeStruct((n, d), F32), jax.ShapeDtypeStruct(state_shape, F32)],
        scratch_shapes=[pltpu.VMEM((tm, d), BF16), pltpu.VMEM((tm, d), F32),
                        pltpu.VMEM((CONV_HALO, 2 * d_ff), F32),
                        pltpu.VMEM((2, tm, FFN_SEQ_CHUNK), F32), pltpu.VMEM((2, tm, FFN_SEQ_CHUNK), F32)],
        compiler_params=_params("arbitrary", "arbitrary"),
        name="conv_ffn_seq",
    )(x1, norm_ffn, w_up, conv_w, conv_b, w_down)


PROMPT_TM = 512
FFN_TM = 512
DECODE_PAGES_PER_STEP = 8
ATTN_BLK_Q = 1024
ATTN_BLK_K = 512
ATTN_HEADS_PER_STEP = 2


def kernel(x_prompt, x_sample, cache_k, cache_v, state_pool, state_ffn_conv, page_table, norm_mix, w_in, q_norm, k_norm, lambda_q1, lambda_k1, lambda_q2, lambda_k2, head_norm, w_pool, pool_scale, w_attn_branch, w_pool_branch, w_out, norm_ffn, w_up, conv_w, conv_b, w_down):
    depth = w_in.shape[0]
    b, t, d = x_prompt.shape
    bs, ts, _ = x_sample.shape
    width = N_HEADS * V_DIM
    d_ff = w_down.shape[1]
    past_len = page_table.shape[1] * PAGE_SIZE

    yp, ys = x_prompt, x_sample.reshape(bs * ts, d)
    outs = [[] for _ in range(8)]
    for l in range(depth):
        lam_init = lambda_init_for(l)
        w_in_bf = w_in[l].astype(BF16)
        w_pool_bf = w_pool[l].astype(BF16)
        w_attn_bf = w_attn_branch[l].astype(BF16)
        w_poolb_bf = w_pool_branch[l].astype(BF16)
        w_out_bf = w_out[l].astype(BF16)
        w_up_bf = w_up[l].astype(BF16)
        w_down_bf = w_down[l].astype(BF16)
        nm = norm_mix[l].reshape(1, d)
        nf = norm_ffn[l].reshape(1, d)
        qg = q_norm[l].reshape(1, V_DIM)
        kg = k_norm[l].reshape(1, V_DIM)
        lam_vecs = jnp.stack([lambda_q1[l], lambda_k1[l], lambda_q2[l], lambda_k2[l]])
        ps = pool_scale[l].reshape(1, -1)
        cb = conv_b[l].reshape(1, -1)

        q, kf, kb, vf, vb, u, sa, sp = _in_projection(yp.reshape(b * t, d), nm, w_in_bf, qg, kg, PROMPT_TM)
        r3 = lambda a: a.reshape(b, t, -1)
        o_n = _prompt_attention(lam_vecs, r3(q), r3(kb), vb, head_norm[l], lam_init,
                                ATTN_BLK_Q, ATTN_BLK_K, ATTN_HEADS_PER_STEP)
        x1 = _prompt_mix(r3(u), o_n, r3(sa), r3(sp), yp, w_pool_bf, ps, w_attn_bf, w_poolb_bf, w_out_bf, PROMPT_TM)
        y2d, f_tails = _conv_ffn_seq(x1.reshape(b * t, d), nf, w_up_bf, conv_w[l], cb, w_down_bf, b, FFN_TM)
        yp = y2d.reshape(b, t, d)
        outs[0].append(kf.reshape(b, t, N_HEADS, V_DIM))
        outs[1].append(vf.reshape(b, t, N_HEADS, V_DIM))
        outs[2].append(r3(u)[:, t - POOL_BUF:, :])
        last = f_tails.reshape((b, -1) + f_tails.shape[1:])[:, -1]
        outs[3].append(last[:, CONV_HALO - (CONV_W - 1):, :])

        n_s = bs * ts
        q, kf, kb, vf, vb, u, sa, sp = _in_projection(ys, nm, w_in_bf, qg, kg, n_s)
        s3 = lambda a: a.reshape(bs, ts, -1)
        o_n = _decode_attention(page_table, lam_vecs, s3(q),
                                kf.reshape(bs, ts * N_HEADS, V_DIM), vf.reshape(bs, ts * N_HEADS, V_DIM),
                                cache_k[l], cache_v[l], head_norm[l], lam_init)
        pool_hist = jnp.pad(state_pool[l], ((0, 0), (POOL_HALO - POOL_BUF, 0), (0, 0)))
        x1, pool_state = _sample_mix(u, pool_hist, o_n.reshape(n_s, width), sa, sp, ys, w_pool_bf, ps,
                                     w_attn_bf, w_poolb_bf, w_out_bf, ts, past_len)
        conv_hist = jnp.pad(state_ffn_conv[l], ((0, 0), (CONV_HALO - (CONV_W - 1), 0), (0, 0)))
        conv_hist = conv_hist.reshape(bs * CONV_HALO, 2 * d_ff)
        ys, fg, fv = _conv_ffn(x1, nf, w_up_bf, conv_w[l], cb, w_down_bf,
                               conv_hist[:, :d_ff], conv_hist[:, d_ff:], tm=n_s, seq_len=ts)
        outs[4].append(kf.reshape(bs, ts, N_HEADS, V_DIM))
        outs[5].append(vf.reshape(bs, ts, N_HEADS, V_DIM))
        outs[6].append(pool_state[:, POOL_HALO - POOL_BUF:, :])
        ffn_state = jnp.concatenate([fg, fv], axis=-1).reshape(bs, ts, 2 * d_ff)
        outs[7].append(ffn_state[:, ts - (CONV_W - 1):, :])

    return (yp, ys.reshape(bs, ts, d)) + tuple(jnp.stack(o) for o in outs)
```

```python
import functools
import math

import jax
import jax.numpy as jnp
from jax import lax
from jax.experimental import pallas as pl
from jax.experimental.pallas import tpu as pltpu

F32 = jnp.float32
BF16 = jnp.bfloat16

N_HEADS = 8
HEAD_DIM = 64
V_DIM = 2 * HEAD_DIM
POOL_WINDOWS = (2, 4, 8, 16)
POOL_GROUP = 256
POOL_BUF = max(POOL_WINDOWS) - 1
POOL_HALO = 16
CONV_W = 3
CONV_HALO = 8
PAGE_SIZE = 128
SUM_ROWS = 16
EPS = 1e-6
NEG = -1e30
Q_SCALE = HEAD_DIM ** -0.5 * math.log2(math.e)
FF_CHUNK = 256
FFN_SEQ_CHUNK = 1024
FFN_ROW_PARTS = 1
IN_SECTION = 1024
IN_PART_ROWS = 256
VMEM_LIMIT = 56 * 1024 * 1024


def lambda_init_for(layer):
    return 0.8 - 0.6 * math.exp(-0.3 * layer)


def _params(*sem):
    return pltpu.CompilerParams(dimension_semantics=sem, vmem_limit_bytes=VMEM_LIMIT)


def _sigmoid(x):
    return 1.0 / (1.0 + jnp.exp(-x))


def _rms_rows(x, g):
    return x * lax.rsqrt(jnp.mean(x * x, axis=-1, keepdims=True) + EPS) * g


def _lambda_value(lam_ref, lam_init):
    lv = lam_ref[...]
    s1 = jnp.sum(lv[0:1] * lv[1:2], axis=-1, keepdims=True)
    s2 = jnp.sum(lv[2:3] * lv[3:4], axis=-1, keepdims=True)
    return jnp.exp(s1) - jnp.exp(s2) + lam_init


def _inproj_kernel(x_ref, nm_ref, w_ref, qg_ref, kg_ref,
                   q_ref, kf_ref, kb_ref, vf_ref, vb_ref, u_ref, sa_ref, sp_ref, h_ref, *, n_split):
    part = x_ref.shape[0] // n_split
    h_ref[...] = _rms_rows(x_ref[...], nm_ref[...]).astype(BF16)

    def section(j, epilogue):
        w = w_ref[:, j * IN_SECTION:(j + 1) * IN_SECTION]
        for r in range(n_split):
            rows = slice(r * part, (r + 1) * part)
            epilogue(rows, jnp.dot(h_ref[rows, :], w, preferred_element_type=F32))

    def group_norm(y, g_ref):
        r = lax.broadcasted_iota(jnp.int32, (V_DIM, V_DIM), 0) // HEAD_DIM
        c = lax.broadcasted_iota(jnp.int32, (V_DIM, V_DIM), 1) // HEAD_DIM
        ones_bd = (r == c).astype(BF16)
        parts = []
        for h in range(N_HEADS):
            yh = y[:, h * V_DIM:(h + 1) * V_DIM]
            ss = jnp.dot((yh * yh).astype(BF16), ones_bd, preferred_element_type=F32)
            parts.append(yh * lax.rsqrt(ss * (1.0 / HEAD_DIM) + EPS) * g_ref[...])
        return jnp.concatenate(parts, axis=-1)

    def store_heads(dst_ref, rows, val):
        for h in range(N_HEADS):
            dst_ref[rows, h, :] = val[:, h * V_DIM:(h + 1) * V_DIM]

    def q_epilogue(rows, y):
        q_ref[rows, :] = (group_norm(y, qg_ref) * Q_SCALE).astype(BF16)

    def k_epilogue(rows, y):
        kn = group_norm(y, kg_ref)
        store_heads(kf_ref, rows, kn)
        kb_ref[rows, :] = kn.astype(BF16)

    def v_epilogue(rows, y):
        store_heads(vf_ref, rows, y)
        vb_ref[:, rows] = y.T.astype(BF16)

    def u_epilogue(rows, y):
        u_ref[rows, :] = y

    def gate_epilogue(dst_ref):
        def epilogue(rows, y):
            dst_ref[rows, :] = _sigmoid(y).astype(BF16)
        return epilogue

    for j, epilogue in enumerate((q_epilogue, k_epilogue, v_epilogue, u_epilogue,
                                  gate_epilogue(sa_ref), gate_epilogue(sp_ref))):
        section(j, epilogue)


def _in_projection(x2d, norm_mix, w_in_bf, q_gain, k_gain, tm):
    n, d = x2d.shape
    row = lambda i: (i, 0)
    fixed = lambda i: (0, 0)
    tile = pl.BlockSpec((tm, IN_SECTION), row)
    heads_tile = pl.BlockSpec((tm, N_HEADS, V_DIM), lambda i: (i, 0, 0))
    flat = lambda dt: jax.ShapeDtypeStruct((n, IN_SECTION), dt)
    heads = jax.ShapeDtypeStruct((n, N_HEADS, V_DIM), F32)
    v_t_tile = pl.BlockSpec((IN_SECTION, tm), lambda i: (0, i))
    v_t = jax.ShapeDtypeStruct((IN_SECTION, n), BF16)
    out_specs = [tile, heads_tile, tile, heads_tile, v_t_tile, tile, tile, tile]
    out_shape = [flat(BF16), heads, flat(BF16), heads, v_t, flat(F32), flat(BF16), flat(BF16)]
    return pl.pallas_call(
        functools.partial(_inproj_kernel, n_split=max(1, tm // IN_PART_ROWS)),
        grid=(n // tm,),
        in_specs=[pl.BlockSpec((tm, d), row),
                  pl.BlockSpec((1, d), fixed),
                  pl.BlockSpec(w_in_bf.shape, fixed, pipeline_mode=pl.Buffered(1)),
                  pl.BlockSpec((1, V_DIM), fixed),
                  pl.BlockSpec((1, V_DIM), fixed)],
        out_specs=out_specs,
        out_shape=out_shape,
        scratch_shapes=[pltpu.VMEM((tm, d), BF16)],
        compiler_params=_params("arbitrary"),
        name="in_projection",
    )(x2d, norm_mix, w_in_bf, q_gain, k_gain)


def _prompt_attn_kernel(lam_ref, q_ref, k_ref, vt_ref, hn_ref, o_ref, m_ref, acc_ref, st0_ref, st1_ref,
                        *, blk_q, blk_k, n_hd, lam_init):
    i = pl.program_id(2)
    lane = lax.broadcasted_iota(jnp.int32, (blk_q, V_DIM), 1)
    q_comp = []
    for hd in range(n_hd):
        q = q_ref[0, :, hd * V_DIM:(hd + 1) * V_DIM]
        zero = jnp.zeros_like(q)
        q_comp.append((jnp.where(lane < HEAD_DIM, q, zero), jnp.where(lane >= HEAD_DIM, q, zero)))

    m_ref[...] = jnp.full(m_ref.shape, NEG, F32)
    acc_ref[...] = jnp.zeros(acc_ref.shape, F32)
    ones_rows = jnp.ones((SUM_ROWS, blk_k), BF16)

    def scores(j, st_ref, q_lo=0):
        start = pl.multiple_of(j * blk_k, blk_k)
        for hd in range(n_hd):
            k = k_ref[0, pl.ds(start, blk_k), hd * V_DIM:(hd + 1) * V_DIM]
            for c in range(2):
                st_ref[hd, c, :, q_lo:] = lax.dot_general(k, q_comp[hd][c][q_lo:], (((1,), (1,)), ((), ())),
                                                          preferred_element_type=F32)

    def update(j, st_ref, masked, q_lo=0):
        start = pl.multiple_of(j * blk_k, blk_k)
        for hd in range(n_hd):
            vt = jnp.concatenate([vt_ref[hd * V_DIM:(hd + 1) * V_DIM, pl.ds(start, blk_k)], ones_rows], axis=0)
            for c in range(2):
                st = st_ref[hd, c, :, q_lo:]
                if masked:
                    k_idx = lax.broadcasted_iota(jnp.int32, (blk_k, blk_k), 0)
                    q_idx = lax.broadcasted_iota(jnp.int32, (blk_k, blk_k), 1)
                    causal = jnp.where(k_idx <= q_idx, st[:, :blk_k], NEG)
                    st = causal if st.shape[1] == blk_k else jnp.concatenate([causal, st[:, blk_k:]], axis=1)
                m_old = m_ref[hd, c, :, q_lo:]
                m_new = jnp.maximum(m_old, jnp.max(st, axis=0, keepdims=True))
                pt = jnp.exp2(st - m_new)
                alpha = jnp.exp2(m_old - m_new)
                acc_ref[hd, c, :, q_lo:] = (alpha * acc_ref[hd, c, :, q_lo:]
                                            + jnp.dot(vt, pt.astype(BF16), preferred_element_type=F32))
                m_ref[hd, c, :, q_lo:] = m_new

    def body(jj, carry):
        scores(2 * jj + 1, st1_ref)
        update(2 * jj, st0_ref, False)
        scores(2 * jj + 2, st0_ref)
        update(2 * jj + 1, st1_ref, False)
        return carry

    assert blk_q == 2 * blk_k
    scores(0, st0_ref)
    lax.fori_loop(0, i, body, 0)
    scores(2 * i + 1, st1_ref, q_lo=blk_k)
    update(2 * i, st0_ref, True)
    update(2 * i + 1, st1_ref, True, q_lo=blk_k)

    lam = _lambda_value(lam_ref, lam_init)
    norm = lambda a: a[:V_DIM] / a[V_DIM:V_DIM + 1]
    for hd in range(n_hd):
        ot = norm(acc_ref[hd, 0]) - lam * norm(acc_ref[hd, 1])
        gain = hn_ref[hd * V_DIM:(hd + 1) * V_DIM, :]
        ot = ot * lax.rsqrt(jnp.mean(ot * ot, axis=0, keepdims=True) + EPS) * gain * (1.0 - lam_init)
        o_ref[0, :, hd * V_DIM:(hd + 1) * V_DIM] = ot.T.astype(BF16)


def _prompt_attention(lam_vecs, q, k, v_t, head_norm, lam_init, blk_q, blk_k, n_hd):
    b, t, _ = q.shape
    q_spec = pl.BlockSpec((1, blk_q, n_hd * V_DIM), lambda bi, h, i: (bi, i, h))
    return pl.pallas_call(
        functools.partial(_prompt_attn_kernel, blk_q=blk_q, blk_k=blk_k, n_hd=n_hd, lam_init=lam_init),
        grid=(b, N_HEADS // n_hd, t // blk_q),
        in_specs=[pl.BlockSpec(lam_vecs.shape, lambda bi, h, i: (0, 0)),
                  q_spec,
                  pl.BlockSpec((1, t, n_hd * V_DIM), lambda bi, h, i: (bi, 0, h)),
                  pl.BlockSpec((n_hd * V_DIM, t), lambda bi, h, i: (h, bi)),
                  pl.BlockSpec((n_hd * V_DIM, 1), lambda bi, h, i: (h, 0))],
        out_specs=q_spec,
        out_shape=jax.ShapeDtypeStruct(q.shape, BF16),
        scratch_shapes=[pltpu.VMEM((n_hd, 2, 1, blk_q), F32),
                        pltpu.VMEM((n_hd, 2, V_DIM + SUM_ROWS, blk_q), F32),
                        pltpu.VMEM((n_hd, 2, blk_k, blk_q), F32), pltpu.VMEM((n_hd, 2, blk_k, blk_q), F32)],
        compiler_params=_params("arbitrary", "arbitrary", "arbitrary"),
        name="prompt_attention",
    )(lam_vecs, q, k, v_t, head_norm.reshape(N_HEADS * V_DIM, 1))


def _decode_attn_kernel(pt_ref, lam_ref, q_ref, kn_ref, vn_ref, *refs, n_new, n_pg, lam_init):
    del pt_ref
    ck_refs, cv_refs = refs[:n_pg], refs[n_pg:2 * n_pg]
    hn_ref, o_ref, a_ref, m_ref, l_ref, acc_ref = refs[2 * n_pg:]
    p_idx = pl.program_id(1)
    n_cols = 2 * N_HEADS * n_new
    cols_per_head = 2 * n_new

    def head_of(shape, row_axis, col_axis):
        r = lax.broadcasted_iota(jnp.int32, shape, row_axis)
        c = lax.broadcasted_iota(jnp.int32, shape, col_axis)
        return r, c, c // cols_per_head

    def update(blocks, causal):
        h_idx, c_idx, c_head = head_of((N_HEADS, n_cols), 0, 1)
        own_head = h_idx == c_head
        raw = [lax.dot_general(k2d, a_ref[...], (((1,), (1,)), ((), ())), preferred_element_type=F32)
               for k2d, _ in blocks]
        scores, valids = [], []
        for s in raw:
            n_tok = s.shape[0] // N_HEADS
            s3 = s.reshape(n_tok, N_HEADS, n_cols)
            valid = own_head[None]
            if causal:
                t_idx = lax.broadcasted_iota(jnp.int32, s3.shape, 0)
                q_idx = lax.broadcasted_iota(jnp.int32, s3.shape, 2) % n_new
                valid = valid & (t_idx <= q_idx)
            scores.append(s3)
            valids.append(valid)
        m_old = m_ref[...]
        m_new = m_old
        for s3, valid in zip(scores, valids):
            m_new = jnp.maximum(m_new, jnp.max(jnp.where(valid, s3, NEG), axis=0))
        alpha = jnp.exp2(m_old - m_new)
        l_new = alpha * l_ref[...]
        pv = None
        for (_, v2d), s3, valid in zip(blocks, scores, valids):
            p3 = jnp.where(valid, jnp.exp2(s3 - m_new[None]), 0.0)
            l_new = l_new + jnp.sum(p3, axis=0)
            part = lax.dot_general(v2d, p3.reshape(v2d.shape[0], n_cols), (((0,), (0,)), ((), ())),
                                   preferred_element_type=F32)
            pv = part if pv is None else pv + part
        l_ref[...] = l_new
        m_ref[...] = m_new
        alpha_row = jnp.sum(jnp.where(own_head, alpha, 0.0), axis=0, keepdims=True)
        acc_ref[...] = acc_ref[...] * alpha_row + pv

    @pl.when(p_idx == 0)
    def _():
        q = q_ref[0].astype(F32)
        lane = lax.broadcasted_iota(jnp.int32, (n_new, V_DIM), 1)
        rows = []
        for h in range(N_HEADS):
            qh = q[:, h * V_DIM:(h + 1) * V_DIM]
            rows += [jnp.where(lane < HEAD_DIM, qh, 0.0), jnp.where(lane >= HEAD_DIM, qh, 0.0)]
        a_ref[...] = jnp.concatenate(rows, axis=0)
        m_ref[...] = jnp.full(m_ref.shape, NEG, F32)
        l_ref[...] = jnp.zeros(l_ref.shape, F32)
        acc_ref[...] = jnp.zeros(acc_ref.shape, F32)
        update([(kn_ref[0], vn_ref[0])], True)

    rows = PAGE_SIZE * N_HEADS
    pages = [(ck[0].reshape(rows, V_DIM), cv[0].reshape(rows, V_DIM)) for ck, cv in zip(ck_refs, cv_refs)]
    update(pages, False)

    @pl.when(p_idx == pl.num_programs(1) - 1)
    def _():
        lam = _lambda_value(lam_ref, lam_init)
        h_idx, c_idx, c_head = head_of((N_HEADS, n_cols), 0, 1)
        l_row = jnp.sum(jnp.where(h_idx == c_head, l_ref[...], 0.0), axis=0, keepdims=True)
        on = (acc_ref[...] / l_row).T
        parts = []
        for h in range(N_HEADS):
            blk = on[h * cols_per_head:(h + 1) * cols_per_head]
            o = blk[:n_new] - lam * blk[n_new:]
            parts.append(_rms_rows(o, hn_ref[:, h * V_DIM:(h + 1) * V_DIM]) * (1.0 - lam_init))
        o_ref[0] = jnp.concatenate(parts, axis=-1).astype(BF16)


def _decode_attention(page_table, lam_vecs, q, k_new, v_new, cache_k, cache_v, head_norm, lam_init):
    b, n_new, width = q.shape
    n_pages = page_table.shape[1]
    n_cols = 2 * N_HEADS * n_new
    seq_spec = pl.BlockSpec((1, n_new, width), lambda bi, p, pt: (bi, 0, 0))
    new_spec = pl.BlockSpec((1, n_new * N_HEADS, V_DIM), lambda bi, p, pt: (bi, 0, 0))
    n_pg = DECODE_PAGES_PER_STEP
    page_specs = [pl.BlockSpec((1, PAGE_SIZE, N_HEADS, V_DIM),
                               lambda bi, p, pt, s=s: (pt[bi, p * n_pg + s], 0, 0, 0)) for s in range(n_pg)]
    grid_spec = pltpu.PrefetchScalarGridSpec(
        num_scalar_prefetch=1,
        grid=(b, n_pages // n_pg),
        in_specs=[pl.BlockSpec(lam_vecs.shape, lambda bi, p, pt: (0, 0)),
                  seq_spec, new_spec, new_spec, *page_specs, *page_specs,
                  pl.BlockSpec((1, width), lambda bi, p, pt: (0, 0))],
        out_specs=seq_spec,
        scratch_shapes=[pltpu.VMEM((n_cols, V_DIM), F32), pltpu.VMEM((N_HEADS, n_cols), F32),
                        pltpu.VMEM((N_HEADS, n_cols), F32), pltpu.VMEM((V_DIM, n_cols), F32)],
    )
    return pl.pallas_call(
        functools.partial(_decode_attn_kernel, n_new=n_new, n_pg=n_pg, lam_init=lam_init),
        grid_spec=grid_spec,
        out_shape=jax.ShapeDtypeStruct(q.shape, BF16),
        compiler_params=_params("arbitrary", "arbitrary"),
        name="decode_attention",
    )(page_table, lam_vecs, q, k_new, v_new, *([cache_k] * n_pg), *([cache_v] * n_pg),
      head_norm.reshape(1, width))


def _pool_branch(ext, n_seq, seq_len, pos, wp_ref, ps_ref):
    rows = POOL_HALO + seq_len

    def tokens(a):
        if n_seq == 1:
            return a[POOL_HALO:]
        return a.reshape(n_seq, rows, a.shape[-1])[:, POOL_HALO:, :].reshape(n_seq * seq_len, a.shape[-1])

    cur = ext
    outs = []
    for g, w in enumerate(POOL_WINDOWS):
        cur = cur + pltpu.roll(cur, w // 2, axis=0)
        c0 = g * POOL_GROUP
        win = tokens(cur[:, :POOL_GROUP])
        tok = tokens(ext[:, c0:c0 + POOL_GROUP])
        cnt = jnp.minimum(w, pos + 1).astype(F32)
        d = (win / cnt - tok).astype(BF16)
        outs.append(jnp.dot(d, wp_ref[g], preferred_element_type=F32) * ps_ref[:, c0:c0 + POOL_GROUP])
        cur = cur[:, POOL_GROUP:]
    return jnp.concatenate(outs, axis=-1)


def _merge_project(x, o_n, pool_o, sig_a, sig_p, wa_ref, wpb_ref, wo_ref):
    attn = jnp.dot(o_n, wa_ref[...], preferred_element_type=F32)
    pool = jnp.dot(pool_o.astype(BF16), wpb_ref[...], preferred_element_type=F32)
    merged = sig_a * attn + sig_p * pool
    return x + jnp.dot(merged.astype(BF16), wo_ref[...], preferred_element_type=F32)


def _prompt_mix_kernel(u_ref, halo_ref, on_ref, sa_ref, sp_ref, x_ref, wp_ref, ps_ref, wa_ref, wpb_ref, wo_ref,
                       x1_ref, *, tm):
    i = pl.program_id(1)
    halo = jnp.where(i > 0, halo_ref[0], 0.0)
    ext = jnp.concatenate([halo, u_ref[0]], axis=0)
    pos = i * tm + lax.broadcasted_iota(jnp.int32, (tm, 1), 0)
    pool_o = _pool_branch(ext, 1, tm, pos, wp_ref, ps_ref)
    x1_ref[0] = _merge_project(x_ref[0], on_ref[0], pool_o, sa_ref[0], sp_ref[0], wa_ref, wpb_ref, wo_ref)


def _prompt_mix(u, o_n, sig_a, sig_p, x, w_pool, pool_scale, w_attn, w_poolb, w_out, tm):
    b, t, d = x.shape
    tile = lambda bi, i: (bi, i, 0)
    fixed2 = lambda bi, i: (0, 0)
    halo_blocks = tm // POOL_HALO
    tile_spec = pl.BlockSpec((1, tm, d), tile)
    return pl.pallas_call(
        functools.partial(_prompt_mix_kernel, tm=tm),
        grid=(b, t // tm),
        in_specs=[tile_spec,
                  pl.BlockSpec((1, POOL_HALO, d), lambda bi, i: (bi, jnp.maximum(i * halo_blocks - 1, 0), 0)),
                  tile_spec, tile_spec, tile_spec, tile_spec,
                  pl.BlockSpec(w_pool.shape, lambda bi, i: (0, 0, 0)),
                  pl.BlockSpec((1, d), fixed2),
                  pl.BlockSpec(w_attn.shape, fixed2),
                  pl.BlockSpec(w_poolb.shape, fixed2),
                  pl.BlockSpec(w_out.shape, fixed2)],
        out_specs=tile_spec,
        out_shape=jax.ShapeDtypeStruct(x.shape, F32),
        compiler_params=_params("arbitrary", "arbitrary"),
        name="prompt_mix",
    )(u, u, o_n, sig_a, sig_p, x, w_pool, pool_scale, w_attn, w_poolb, w_out)


def _sample_mix_kernel(u_ref, hist_ref, on_ref, sa_ref, sp_ref, x_ref, wp_ref, ps_ref, wa_ref, wpb_ref, wo_ref,
                       x1_ref, state_ref, *, n_seq, seq_len, pos0):
    d = u_ref.shape[-1]
    rows = POOL_HALO + seq_len
    ext3 = jnp.concatenate([hist_ref[...], u_ref[...].reshape(n_seq, seq_len, d)], axis=1)
    state_ref[...] = ext3[:, rows - POOL_HALO:, :]
    pos = pos0 + lax.broadcasted_iota(jnp.int32, (n_seq * seq_len, 1), 0) % seq_len
    pool_o = _pool_branch(ext3.reshape(n_seq * rows, d), n_seq, seq_len, pos, wp_ref, ps_ref)
    x1_ref[...] = _merge_project(x_ref[...], on_ref[...], pool_o, sa_ref[...], sp_ref[...], wa_ref, wpb_ref, wo_ref)


def _sample_mix(u, hist, o_n, sig_a, sig_p, x, w_pool, pool_scale, w_attn, w_poolb, w_out, seq_len, pos0):
    n, d = x.shape
    n_seq = n // seq_len
    full = lambda a: pl.BlockSpec(a.shape, lambda i, nd=a.ndim: (0,) * nd)
    args = (u, hist, o_n, sig_a, sig_p, x, w_pool, pool_scale, w_attn, w_poolb, w_out)
    out_shape = [jax.ShapeDtypeStruct(x.shape, F32), jax.ShapeDtypeStruct(hist.shape, F32)]
    return pl.pallas_call(
        functools.partial(_sample_mix_kernel, n_seq=n_seq, seq_len=seq_len, pos0=pos0),
        grid=(1,),
        in_specs=[full(a) for a in args],
        out_specs=[full(s) for s in out_shape],
        out_shape=out_shape,
        compiler_params=_params("arbitrary"),
        name="sample_mix",
    )(*args)


def _conv_taps(up, hist, n_seq, seq_len):
    c = up.shape[-1]
    rows = CONV_HALO + seq_len
    if n_seq == 1:
        ext = jnp.concatenate([hist, up], axis=0)
    else:
        ext = jnp.concatenate([hist.reshape(n_seq, CONV_HALO, c), up.reshape(n_seq, seq_len, c)], axis=1)
        ext = ext.reshape(n_seq * rows, c)

    def tokens(a):
        if n_seq == 1:
            return a[CONV_HALO:]
        return a.reshape(n_seq, rows, c)[:, CONV_HALO:, :].reshape(n_seq * seq_len, c)

    return tokens(pltpu.roll(ext, 1, axis=0)), tokens(pltpu.roll(ext, 2, axis=0))


def _ffn_kernel(x1_ref, nf_ref, wg_ref, wv_ref, cwg_ref, cwv_ref, cbg_ref, cbv_ref, wd_ref, hg_ref, hv_ref,
                y_ref, sg_ref, sv_ref, h_ref, acc_ref, *, n_seq, seq_len):
    j = pl.program_id(1)
    n_j = pl.num_programs(1)

    @pl.when(j == 0)
    def _():
        h_ref[...] = _rms_rows(x1_ref[...], nf_ref[...]).astype(BF16)
        acc_ref[...] = jnp.zeros(acc_ref.shape, F32)

    h = h_ref[...]
    halves = []
    for w_ref, cw_ref, cb_ref, hist_ref, s_ref in ((wg_ref, cwg_ref, cbg_ref, hg_ref, sg_ref),
                                                   (wv_ref, cwv_ref, cbv_ref, hv_ref, sv_ref)):
        up = jnp.dot(h, w_ref[...], preferred_element_type=F32)
        s_ref[...] = up
        prev1, prev2 = _conv_taps(up, hist_ref[...], n_seq, seq_len)
        cw = cw_ref[...]
        halves.append(cb_ref[...] + prev2 * cw[0:1] + prev1 * cw[1:2] + up * cw[2:3])
    gate, val = halves
    act = (gate * _sigmoid(gate) * val).astype(BF16)
    acc_ref[...] += jnp.dot(act, wd_ref[...], preferred_element_type=F32)

    @pl.when(j == n_j - 1)
    def _():
        y_ref[...] = x1_ref[...] + acc_ref[...]


def _conv_ffn(x1, norm_ffn, w_up, conv_w, conv_b, w_down, hist_gate, hist_val, tm, seq_len):
    n, d = x1.shape
    d_ff = w_down.shape[0]
    n_chunks = d_ff // FF_CHUNK
    n_seq = tm // seq_len
    row = lambda i, j: (i, 0)
    gate_col = lambda i, j: (0, j)
    val_col = lambda i, j: (0, n_chunks + j)
    up_shape = jax.ShapeDtypeStruct((n, d_ff), F32)
    up_spec = pl.BlockSpec((tm, FF_CHUNK), lambda i, j: (i, j))
    hist_spec = pl.BlockSpec((n_seq * CONV_HALO, FF_CHUNK), lambda i, j: (i, j))
    return pl.pallas_call(
        functools.partial(_ffn_kernel, n_seq=n_seq, seq_len=seq_len),
        grid=(n // tm, n_chunks),
        in_specs=[pl.BlockSpec((tm, d), row),
                  pl.BlockSpec((1, d), lambda i, j: (0, 0)),
                  pl.BlockSpec((d, FF_CHUNK), gate_col),
                  pl.BlockSpec((d, FF_CHUNK), val_col),
                  pl.BlockSpec((CONV_W, FF_CHUNK), gate_col),
                  pl.BlockSpec((CONV_W, FF_CHUNK), val_col),
                  pl.BlockSpec((1, FF_CHUNK), gate_col),
                  pl.BlockSpec((1, FF_CHUNK), val_col),
                  pl.BlockSpec((FF_CHUNK, d), lambda i, j: (j, 0)),
                  hist_spec, hist_spec],
        out_specs=[pl.BlockSpec((tm, d), row), up_spec, up_spec],
        out_shape=[jax.ShapeDtypeStruct((n, d), F32), up_shape, up_shape],
        scratch_shapes=[pltpu.VMEM((tm, d), BF16), pltpu.VMEM((tm, d), F32)],
        compiler_params=_params("arbitrary", "arbitrary"),
        name="conv_ffn",
    )(x1, norm_ffn, w_up, w_up, conv_w, conv_w, conv_b, conv_b, w_down, hist_gate, hist_val)


def _ffn_seq_kernel(x1_ref, nf_ref, wu_ref, cw_ref, cb_ref, wd_ref, y_ref, state_ref,
                    h_ref, acc_ref, carry_ref, up0_ref, up1_ref):
    i = pl.program_id(1)
    tm = x1_ref.shape[0]
    d_ff = wd_ref.shape[0]
    bounds = list(range(0, d_ff, FFN_SEQ_CHUNK)) + [d_ff]
    n_chunks = len(bounds) - 1
    h_ref[...] = _rms_rows(x1_ref[...], nf_ref[...]).astype(BF16)
    acc_ref[...] = jnp.zeros(acc_ref.shape, F32)
    up_refs = (up0_ref, up1_ref)

    def cols(c, half):
        return slice(half * d_ff + bounds[c], half * d_ff + bounds[c + 1])

    def up_project(c):
        width = bounds[c + 1] - bounds[c]
        for half in range(2):
            up_refs[c % 2][half, :, :width] = jnp.dot(h_ref[...], wu_ref[:, cols(c, half)],
                                                      preferred_element_type=F32)

    def consume(c):
        part = tm // FFN_ROW_PARTS
        width = bounds[c + 1] - bounds[c]
        hists = [jnp.where(i > 0, carry_ref[:, cols(c, half)], 0.0) for half in range(2)]
        for r in range(FFN_ROW_PARTS):
            rows = slice(r * part, (r + 1) * part)
            halves = []
            for half in range(2):
                up = up_refs[c % 2][half, rows, :width]
                prev1, prev2 = _conv_taps(up, hists[half], 1, part)
                hists[half] = up[part - CONV_HALO:]
                cw = cw_ref[:, cols(c, half)]
                halves.append(cb_ref[:, cols(c, half)] + prev2 * cw[0:1] + prev1 * cw[1:2] + up * cw[2:3])
            gate, val = halves
            act = (gate * _sigmoid(gate) * val).astype(BF16)
            acc_ref[rows, :] += jnp.dot(act, wd_ref[bounds[c]:bounds[c + 1], :], preferred_element_type=F32)
        for half in range(2):
            carry_ref[:, cols(c, half)] = hists[half]
            state_ref[0, :, cols(c, half)] = hists[half]

    up_project(0)
    for c in range(n_chunks):
        if c + 1 < n_chunks:
            up_project(c + 1)
        consume(c)
    y_ref[...] = x1_ref[...] + acc_ref[...]


def _conv_ffn_seq(x1, norm_ffn, w_up, conv_w, conv_b, w_down, n_batch, tm):
    n, d = x1.shape
    d_ff = w_down.shape[0]
    tiles = n // (n_batch * tm)
    row = lambda bi, i: (bi * tiles + i, 0)
    whole = lambda a: pl.BlockSpec(a.shape, lambda bi, i, nd=a.ndim: (0,) * nd, pipeline_mode=pl.Buffered(1))
    state_shape = (n_batch * tiles, CONV_HALO, 2 * d_ff)
    return pl.pallas_call(
        _ffn_seq_kernel,
        grid=(n_batch, tiles),
        in_specs=[pl.BlockSpec((tm, d), row), whole(norm_ffn), whole(w_up), whole(conv_w), whole(conv_b),
                  whole(w_down)],
        out_specs=[pl.BlockSpec((tm, d), row),
                   pl.BlockSpec((1,) + state_shape[1:], lambda bi, i: (bi * tiles + i, 0, 0))],
        out_shape=[jax.ShapeDtypeStruct((n, d), F32), jax.ShapeDtypeStruct(state_shape, F32)],
        scratch_shapes=[pltpu.VMEM((tm, d), BF16), pltpu.VMEM((tm, d), F32),
                        pltpu.VMEM((CONV_HALO, 2 * d_ff), F32),
                        pltpu.VMEM((2, tm, FFN_SEQ_CHUNK), F32), pltpu.VMEM((2, tm, FFN_SEQ_CHUNK), F32)],
        compiler_params=_params("arbitrary", "arbitrary"),
        name="conv_ffn_seq",
    )(x1, norm_ffn, w_up, conv_w, conv_b, w_down)


PROMPT_TM = 512
FFN_TM = 512
DECODE_PAGES_PER_STEP = 8
ATTN_BLK_Q = 1024
ATTN_BLK_K = 512
ATTN_HEADS_PER_STEP = 2


def kernel(x_prompt, x_sample, cache_k, cache_v, state_pool, state_ffn_conv, page_table, norm_mix, w_in, q_norm, k_norm, lambda_q1, lambda_k1, lambda_q2, lambda_k2, head_norm, w_pool, pool_scale, w_attn_branch, w_pool_branch, w_out, norm_ffn, w_up, conv_w, conv_b, w_down):
    depth = w_in.shape[0]
    b, t, d = x_prompt.shape
    bs, ts, _ = x_sample.shape
    width = N_HEADS * V_DIM
    d_ff = w_down.shape[1]
    past_len = page_table.shape[1] * PAGE_SIZE

    yp, ys = x_prompt, x_sample.reshape(bs * ts, d)
    outs = [[] for _ in range(8)]
    for l in range(depth):
        lam_init = lambda_init_for(l)
        w_in_bf = w_in[l].astype(BF16)
        w_pool_bf = w_pool[l].astype(BF16)
        w_attn_bf = w_attn_branch[l].astype(BF16)
        w_poolb_bf = w_pool_branch[l].astype(BF16)
        w_out_bf = w_out[l].astype(BF16)
        w_up_bf = w_up[l].astype(BF16)
        w_down_bf = w_down[l].astype(BF16)
        nm = norm_mix[l].reshape(1, d)
        nf = norm_ffn[l].reshape(1, d)
        qg = q_norm[l].reshape(1, V_DIM)
        kg = k_norm[l].reshape(1, V_DIM)
        lam_vecs = jnp.stack([lambda_q1[l], lambda_k1[l], lambda_q2[l], lambda_k2[l]])
        ps = pool_scale[l].reshape(1, -1)
        cb = conv_b[l].reshape(1, -1)

        q, kf, kb, vf, vb, u, sa, sp = _in_projection(yp.reshape(b * t, d), nm, w_in_bf, qg, kg, PROMPT_TM)
        r3 = lambda a: a.reshape(b, t, -1)
        o_n = _prompt_attention(lam_vecs, r3(q), r3(kb), vb, head_norm[l], lam_init,
                                ATTN_BLK_Q, ATTN_BLK_K, ATTN_HEADS_PER_STEP)
        x1 = _prompt_mix(r3(u), o_n, r3(sa), r3(sp), yp, w_pool_bf, ps, w_attn_bf, w_poolb_bf, w_out_bf, PROMPT_TM)
        y2d, f_tails = _conv_ffn_seq(x1.reshape(b * t, d), nf, w_up_bf, conv_w[l], cb, w_down_bf, b, FFN_TM)
        yp = y2d.reshape(b, t, d)
        outs[0].append(kf.reshape(b, t, N_HEADS, V_DIM))
        outs[1].append(vf.reshape(b, t, N_HEADS, V_DIM))
        outs[2].append(r3(u)[:, t - POOL_BUF:, :])
        last = f_tails.reshape((b, -1) + f_tails.shape[1:])[:, -1]
        outs[3].append(last[:, CONV_HALO - (CONV_W - 1):, :])

        n_s = bs * ts
        q, kf, kb, vf, vb, u, sa, sp = _in_projection(ys, nm, w_in_bf, qg, kg, n_s)
        s3 = lambda a: a.reshape(bs, ts, -1)
        o_n = _decode_attention(page_table, lam_vecs, s3(q),
                                kf.reshape(bs, ts * N_HEADS, V_DIM), vf.reshape(bs, ts * N_HEADS, V_DIM),
                                cache_k[l], cache_v[l], head_norm[l], lam_init)
        pool_hist = jnp.pad(state_pool[l], ((0, 0), (POOL_HALO - POOL_BUF, 0), (0, 0)))
        x1, pool_state = _sample_mix(u, pool_hist, o_n.reshape(n_s, width), sa, sp, ys, w_pool_bf, ps,
                                     w_attn_bf, w_poolb_bf, w_out_bf, ts, past_len)
        conv_hist = jnp.pad(state_ffn_conv[l], ((0, 0), (CONV_HALO - (CONV_W - 1), 0), (0, 0)))
        conv_hist = conv_hist.reshape(bs * CONV_HALO, 2 * d_ff)
        ys, fg, fv = _conv_ffn(x1, nf, w_up_bf, conv_w[l], cb, w_down_bf,
                               conv_hist[:, :d_ff], conv_hist[:, d_ff:], tm=n_s, seq_len=ts)
        outs[4].append(kf.reshape(bs, ts, N_HEADS, V_DIM))
        outs[5].append(vf.reshape(bs, ts, N_HEADS, V_DIM))
        outs[6].append(pool_state[:, POOL_HALO - POOL_BUF:, :])
        ffn_state = jnp.concatenate([fg, fv], axis=-1).reshape(bs, ts, 2 * d_ff)
        outs[7].append(ffn_state[:, ts - (CONV_W - 1):, :])

    return (yp, ys.reshape(bs, ts, d)) + tuple(jnp.stack(o) for o in outs)
```
